```python
import jax, jax.numpy as jnp
from jax import lax
import numpy as np

D_MODEL = 1024
BATCH = 8
SEQ = 4096
DEPTH = 1
DEC_BATCH = 32
DEC_SEQ = 4
PAST_LEN = 16384
PAGE_SIZE = 128

HEAD_DIM = 64
N_FOX_HEADS = 8
N_SB_HEADS = 8
N_HEADS = N_FOX_HEADS + N_SB_HEADS
MIX_WIDTH = N_HEADS * HEAD_DIM
IN_WIDTH = 3 * MIX_WIDTH + N_FOX_HEADS
Q_BLOCK = 128
N_GROUPS = 4
EXPERTS_PER_GROUP = 8
N_EXPERTS = N_GROUPS * EXPERTS_PER_GROUP
TOP_K_IN_GROUP = 2
D_EXPERT = 256
DEEPNORM_ALPHA = (2 * DEPTH) ** 0.25
DEEPNORM_BETA = (8 * DEPTH) ** -0.25
LN_EPS = 1e-5
RMS_EPS = 1e-6
FORGET_BIAS_INIT = 4.0

kernel_name = "hymba_fox_stickbreak_hmoe_step"


def layer_norm(x, g, b):
    xf = x.astype(jnp.float32)
    mu = jnp.mean(xf, axis=-1, keepdims=True)
    var = jnp.mean(jnp.square(xf - mu), axis=-1, keepdims=True)
    return ((xf - mu) * lax.rsqrt(var + LN_EPS) * g + b).astype(x.dtype)


def adaln(c, w_ada, b_ada):
    mod = jax.nn.silu(c) @ w_ada + b_ada
    return jnp.split(mod[:, None, :], 6, axis=-1)


def project(h, w_in, b_forget):
    b, t, _ = h.shape
    p = h @ w_in
    q = p[..., :MIX_WIDTH].reshape(b, t, N_HEADS, HEAD_DIM)
    k = p[..., MIX_WIDTH:2 * MIX_WIDTH].reshape(b, t, N_HEADS, HEAD_DIM)
    v = p[..., 2 * MIX_WIDTH:3 * MIX_WIDTH].reshape(b, t, N_HEADS, HEAD_DIM)
    logf = jax.nn.log_sigmoid((p[..., 3 * MIX_WIDTH:] + b_forget).astype(jnp.float32))
    return q, k, v, logf


def fox_attend(q, k, v, cq, ck, q_pos, k_pos):
    s = jnp.einsum('bqhd,bkhd->bhqk', q, k).astype(jnp.float32) * (HEAD_DIM ** -0.5)
    bias = jnp.transpose(cq, (0, 2, 1))[..., :, None] - jnp.transpose(ck, (0, 2, 1))[..., None, :]
    mask = k_pos[None, :] <= q_pos[:, None]
    p = jax.nn.softmax(jnp.where(mask, s + bias, -jnp.inf), axis=-1)
    return jnp.einsum('bhqk,bkhd->bqhd', p.astype(v.dtype), v)


def sb_attend(q, k, v, q_pos, k_pos):
    z = jnp.einsum('bqhd,bkhd->bhqk', q, k).astype(jnp.float32) * (HEAD_DIM ** -0.5)
    mask = k_pos[None, :] < q_pos[:, None]
    log_1m = jnp.where(mask, jax.nn.log_sigmoid(-z), 0.0)
    after = lax.cumsum(log_1m, axis=3, reverse=True) - log_1m
    a = jnp.where(mask, jnp.exp(jax.nn.log_sigmoid(z) + after), 0.0)
    return jnp.einsum('bhqk,bkhd->bqhd', a.astype(v.dtype), v)


def merge_heads(o, g_mix, w_out):
    b, t = o.shape[:2]
    of = o.astype(jnp.float32)
    of = of * lax.rsqrt(jnp.mean(jnp.square(of), axis=-1, keepdims=True) + RMS_EPS)
    of = of.reshape(b, t, MIX_WIDTH) * g_mix
    return of.astype(o.dtype) @ w_out


def prompt_mixer(h, w_in, b_forget, g_mix, w_out):
    q, k, v, logf = project(h, w_in, b_forget)
    b, s = h.shape[:2]
    cum = lax.cumsum(logf, axis=1)
    k_pos = jnp.arange(s)
    qf, kf, vf = q[:, :, :N_FOX_HEADS], k[:, :, :N_FOX_HEADS], v[:, :, :N_FOX_HEADS]
    qs, ks, vs = q[:, :, N_FOX_HEADS:], k[:, :, N_FOX_HEADS:], v[:, :, N_FOX_HEADS:]

    def block(i):
        start = i * Q_BLOCK
        q_pos = start + jnp.arange(Q_BLOCK)
        o_f = fox_attend(lax.dynamic_slice_in_dim(qf, start, Q_BLOCK, 1), kf, vf,
                         lax.dynamic_slice_in_dim(cum, start, Q_BLOCK, 1), cum, q_pos, k_pos)
        o_s = sb_attend(lax.dynamic_slice_in_dim(qs, start, Q_BLOCK, 1), ks, vs, q_pos, k_pos)
        return jnp.concatenate([o_f, o_s], axis=2)

    o = lax.map(block, jnp.arange(s // Q_BLOCK))
    o = jnp.moveaxis(o, 0, 1).reshape(b, s, N_HEADS, HEAD_DIM)
    return merge_heads(o, g_mix, w_out), k, v, logf


def sample_mixer(h, k_past, v_past, logf_past, w_in, b_forget, g_mix, w_out):
    q, k, v, logf = project(h, w_in, b_forget)
    past = k_past.shape[1]
    t_new = h.shape[1]
    k_all = jnp.concatenate([k_past.astype(k.dtype), k], axis=1)
    v_all = jnp.concatenate([v_past.astype(v.dtype), v], axis=1)
    logf_all = jnp.concatenate([logf_past.astype(jnp.float32), logf], axis=1)
    cum = lax.cumsum(logf_all, axis=1)
    k_pos = jnp.arange(past + t_new)
    q_pos = past + jnp.arange(t_new)
    o_f = fox_attend(q[:, :, :N_FOX_HEADS], k_all[:, :, :N_FOX_HEADS], v_all[:, :, :N_FOX_HEADS],
                     cum[:, past:], cum, q_pos, k_pos)
    o_s = sb_attend(q[:, :, N_FOX_HEADS:], k_all[:, :, N_FOX_HEADS:], v_all[:, :, N_FOX_HEADS:],
                    q_pos, k_pos)
    o = jnp.concatenate([o_f, o_s], axis=2)
    return merge_heads(o, g_mix, w_out), k, v, logf


def hier_moe(h, w_rg, b_rg, w_re, b_re, w_gate, w_up, w_down):
    b, t, _ = h.shape
    g_prob = jax.nn.softmax((h @ w_rg + b_rg).astype(jnp.float32), axis=-1)
    g_idx = jnp.argmax(g_prob, axis=-1)
    g_w = jnp.max(g_prob, axis=-1)
    e_logits = (h @ w_re + b_re).astype(jnp.float32).reshape(b, t, N_GROUPS, EXPERTS_PER_GROUP)
    e_logits = jnp.take_along_axis(e_logits, g_idx[..., None, None], axis=2)[..., 0, :]
    top_w, top_i = lax.top_k(jax.nn.softmax(e_logits, axis=-1), TOP_K_IN_GROUP)
    top_w = top_w / jnp.sum(top_w, axis=-1, keepdims=True)
    expert_idx = g_idx[..., None] * EXPERTS_PER_GROUP + top_i
    gates = jnp.sum(jax.nn.one_hot(expert_idx, N_EXPERTS, dtype=jnp.float32)
                    * (g_w[..., None] * top_w)[..., None], axis=2)

    def per_sequence(args):
        hs, gs = args
        a = jnp.einsum('td,edf->tef', hs, w_gate)
        u = jnp.einsum('td,edf->tef', hs, w_up)
        zz = jax.nn.silu(a) * u * gs[..., None].astype(hs.dtype)
        return jnp.einsum('tef,efd->td', zz, w_down)

    return lax.map(per_sequence, (h, gates))


def residual_block(x, c, mix_fn, w_ada, b_ada, ln1_g, ln1_b, ln2_g, ln2_b, moe_params):
    sh1, sc1, g1, sh2, sc2, g2 = adaln(c, w_ada, b_ada)
    mix_out, k, v, logf = mix_fn(x * (1 + sc1) + sh1)
    x = layer_norm(DEEPNORM_ALPHA * x + g1 * mix_out, ln1_g, ln1_b)
    x = layer_norm(DEEPNORM_ALPHA * x + g2 * hier_moe(x * (1 + sc2) + sh2, *moe_params), ln2_g, ln2_b)
    return x, k, v, logf


def setup_inputs(seed: int = 0) -> dict:
    key = jax.random.key(seed)
    ks = jax.random.split(key, 32)
    n_pages = PAST_LEN // PAGE_SIZE
    n_used = DEC_BATCH * n_pages
    n_pool = n_used + n_used // 4 + 1

    def nrm(k, shape, scale):
        return scale * jax.random.normal(k, shape, jnp.float32)

    x_prompt = nrm(ks[0], (BATCH, SEQ, D_MODEL), 1.0)
    x_sample = nrm(ks[1], (DEC_BATCH, DEC_SEQ, D_MODEL), 1.0)
    c_prompt = nrm(ks[2], (BATCH, D_MODEL), 1.0)
    c_sample = nrm(ks[3], (DEC_BATCH, D_MODEL), 1.0)
    cache_k = nrm(ks[4], (n_pool, DEPTH, PAGE_SIZE, N_HEADS, HEAD_DIM), 1.0)
    cache_v = nrm(ks[5], (n_pool, DEPTH, PAGE_SIZE, N_HEADS, HEAD_DIM), 1.0)
    cache_logf = jax.nn.log_sigmoid(FORGET_BIAS_INIT + nrm(ks[6], (n_pool, DEPTH, PAGE_SIZE, N_FOX_HEADS), 1.0))
    page_table = jax.random.permutation(ks[7], n_pool)[:n_used].reshape(DEC_BATCH, n_pages).astype(jnp.int32)

    w_in = nrm(ks[8], (DEPTH, D_MODEL, IN_WIDTH), D_MODEL ** -0.5)
    w_in = w_in.at[:, :, 2 * MIX_WIDTH:3 * MIX_WIDTH].multiply(DEEPNORM_BETA)
    b_forget = FORGET_BIAS_INIT + nrm(ks[9], (DEPTH, N_FOX_HEADS), 0.5)
    g_mix = 1.0 + nrm(ks[10], (DEPTH, MIX_WIDTH), 0.05)
    w_out = nrm(ks[11], (DEPTH, MIX_WIDTH, D_MODEL), DEEPNORM_BETA * MIX_WIDTH ** -0.5)
    w_ada = nrm(ks[12], (DEPTH, D_MODEL, 6 * D_MODEL), 0.5 * D_MODEL ** -0.5)
    b_ada = nrm(ks[13], (DEPTH, 6 * D_MODEL), 0.02)
    ln1_g = 1.0 + nrm(ks[14], (DEPTH, D_MODEL), 0.05)
    ln1_b = nrm(ks[15], (DEPTH, D_MODEL), 0.02)
    ln2_g = 1.0 + nrm(ks[16], (DEPTH, D_MODEL), 0.05)
    ln2_b = nrm(ks[17], (DEPTH, D_MODEL), 0.02)
    w_router_group = nrm(ks[18], (DEPTH, D_MODEL, N_GROUPS), D_MODEL ** -0.5)
    b_router_group = nrm(ks[19], (DEPTH, N_GROUPS), 0.01)
    w_router_expert = nrm(ks[20], (DEPTH, D_MODEL, N_EXPERTS), D_MODEL ** -0.5)
    b_router_expert = nrm(ks[21], (DEPTH, N_EXPERTS), 0.01)
    w_gate_e = nrm(ks[22], (DEPTH, N_EXPERTS, D_MODEL, D_EXPERT), D_MODEL ** -0.5)
    w_up_e = nrm(ks[23], (DEPTH, N_EXPERTS, D_MODEL, D_EXPERT), D_MODEL ** -0.5)
    w_down_e = nrm(ks[24], (DEPTH, N_EXPERTS, D_EXPERT, D_MODEL), DEEPNORM_BETA * D_EXPERT ** -0.5)
    return {"x_prompt": x_prompt, "x_sample": x_sample, "c_prompt": c_prompt, "c_sample": c_sample,
            "cache_k": cache_k, "cache_v": cache_v, "cache_logf": cache_logf, "page_table": page_table,
            "w_in": w_in, "b_forget": b_forget, "g_mix": g_mix, "w_out": w_out,
            "w_ada": w_ada, "b_ada": b_ada, "ln1_g": ln1_g, "ln1_b": ln1_b, "ln2_g": ln2_g, "ln2_b": ln2_b,
            "w_router_group": w_router_group, "b_router_group": b_router_group,
            "w_router_expert": w_router_expert, "b_router_expert": b_router_expert,
            "w_gate_e": w_gate_e, "w_up_e": w_up_e, "w_down_e": w_down_e}


def reference(x_prompt, x_sample, c_prompt, c_sample, cache_k, cache_v, cache_logf, page_table,
              w_in, b_forget, g_mix, w_out, w_ada, b_ada, ln1_g, ln1_b, ln2_g, ln2_b,
              w_router_group, b_router_group, w_router_expert, b_router_expert,
              w_gate_e, w_up_e, w_down_e):
    db, n_pages = page_table.shape
    past = n_pages * PAGE_SIZE
    yp, ys = x_prompt, x_sample
    kp_l, vp_l, lp_l, ks_l, vs_l, ls_l = [], [], [], [], [], []
    for l in range(DEPTH):
        moe_l = (w_router_group[l], b_router_group[l], w_router_expert[l], b_router_expert[l],
                 w_gate_e[l], w_up_e[l], w_down_e[l])
        yp, kp, vp, lp = residual_block(
            yp, c_prompt, lambda h: prompt_mixer(h, w_in[l], b_forget[l], g_mix[l], w_out[l]),
            w_ada[l], b_ada[l], ln1_g[l], ln1_b[l], ln2_g[l], ln2_b[l], moe_l)
        k_past = cache_k[page_table, l].reshape(db, past, N_HEADS, HEAD_DIM)
        v_past = cache_v[page_table, l].reshape(db, past, N_HEADS, HEAD_DIM)
        lf_past = cache_logf[page_table, l].reshape(db, past, N_FOX_HEADS)
        ys, kn, vn, ln_ = residual_block(
            ys, c_sample,
            lambda h: sample_mixer(h, k_past, v_past, lf_past, w_in[l], b_forget[l], g_mix[l], w_out[l]),
            w_ada[l], b_ada[l], ln1_g[l], ln1_b[l], ln2_g[l], ln2_b[l], moe_l)
        kp_l.append(kp); vp_l.append(vp); lp_l.append(lp)
        ks_l.append(kn); vs_l.append(vn); ls_l.append(ln_)
    k_prompt = jnp.stack(kp_l, axis=1)
    v_prompt = jnp.stack(vp_l, axis=1)
    logf_prompt = jnp.stack(lp_l, axis=1)
    k_sample = jnp.stack(ks_l, axis=1)
    v_sample = jnp.stack(vs_l, axis=1)
    logf_sample = jnp.stack(ls_l, axis=1)
    return (yp, ys, k_prompt, v_prompt, logf_prompt, k_sample, v_sample, logf_sample)
```

```python
import functools

import jax
import jax.numpy as jnp
from jax import lax
from jax.experimental import pallas as pl
from jax.experimental.pallas import tpu as pltpu

F32 = jnp.float32
BF16 = jnp.bfloat16
LN_EPS = 1e-5
RMS_EPS = 1e-6
TOP_K_IN_GROUP = 2

V7X_LANES = 128
V7X_SUBLANES = 8
V7X_BF16_SUBLANES = 16
V7X_VMEM_BUDGET = 56 * 1024 * 1024

_NT = (((1,), (1,)), ((), ()))


def _dot(a, b):
    return jnp.dot(a, b, preferred_element_type=F32)


def _dot_nt(a, b):
    return lax.dot_general(a, b, _NT, preferred_element_type=F32)


def _round_up(n, m):
    return (n + m - 1) // m * m


def _split3(x):
    hi = x.astype(BF16)
    r1 = x - hi.astype(F32)
    mid = r1.astype(BF16)
    lo = (r1 - mid.astype(F32)).astype(BF16)
    return hi, mid, lo


def _split2(x):
    hi = x.astype(BF16)
    lo = (x - hi.astype(F32)).astype(BF16)
    return hi, lo


def _softplus_neg_abs(z):
    return jnp.log(1.0 + jnp.exp(-jnp.abs(z)))


def _layer_norm(x, g, b):
    mu = jnp.mean(x, axis=-1, keepdims=True)
    xc = x - mu
    var = jnp.mean(xc * xc, axis=-1, keepdims=True)
    return xc * lax.rsqrt(var + LN_EPS) * g + b


def _adaln_kernel(c_ref, w_ref, b_ref, o_ref):
    c = c_ref[...]
    sc = c * (1.0 / (1.0 + jnp.exp(-c)))
    o_ref[...] = jnp.dot(sc, w_ref[...], precision=lax.Precision.HIGHEST,
                         preferred_element_type=F32) + b_ref[...]


def _adaln(c_all, w_ada, b_ada):
    rows, d = c_all.shape
    n = w_ada.shape[1]
    tn = min(n, 512)
    return pl.pallas_call(
        _adaln_kernel,
        out_shape=jax.ShapeDtypeStruct((rows, n), F32),
        grid=(n // tn,),
        in_specs=[pl.BlockSpec((rows, d), lambda j: (0, 0)),
                  pl.BlockSpec((d, tn), lambda j: (0, j)),
                  pl.BlockSpec((1, tn), lambda j: (0, j))],
        out_specs=pl.BlockSpec((rows, tn), lambda j: (0, j)),
        compiler_params=pltpu.CompilerParams(dimension_semantics=("arbitrary",)),
        name="adaln",
    )(c_all, w_ada, b_ada.reshape(1, n))


def _inproj_kernel(x_ref, sc_ref, sh_ref, w_ref, wf_ref, wft_ref, bf_ref, bft_ref, tri_ref,
                   q_ref, k_ref, v_ref, kb_ref, vb_ref, lf_ref, cum_ref, carry_ref,
                   *, mix, nf, q_scale):
    si = pl.program_id(1)
    h = (x_ref[0] * (1.0 + sc_ref[0]) + sh_ref[0]).astype(BF16)
    q_ref[0] = (_dot(h, w_ref[:, 0:mix]) * q_scale).astype(BF16)
    k = _dot(h, w_ref[:, mix:2 * mix])
    k_ref[0] = k
    kb_ref[0] = k.astype(BF16)
    v = _dot(h, w_ref[:, 2 * mix:3 * mix])
    v_ref[0] = v
    vb_ref[0] = v.astype(BF16)

    logit = _dot(h, wf_ref[...])[:, 0:nf] + bf_ref[...]
    lf_ref[0] = jnp.minimum(logit, 0.0) - _softplus_neg_abs(logit)

    logit_t = _dot_nt(wft_ref[...], h) + bft_ref[...]
    lf_t = jnp.minimum(logit_t, 0.0) - _softplus_neg_abs(logit_t)
    hi, mid, lo = _split3(lf_t)
    tri = tri_ref[...]
    local = _dot(hi, tri) + _dot(mid, tri) + _dot(lo, tri)

    @pl.when(si == 0)
    def _():
        carry_ref[...] = jnp.zeros_like(carry_ref)

    cum = local + carry_ref[:, 0:1]
    cum_ref[0] = cum
    ts = cum.shape[1]
    carry_ref[...] = jnp.broadcast_to(cum[:, ts - 1:ts], carry_ref.shape)


def _inproj(x, sc, sh, w_qkv, wf, wft, bf, bft, *, mix, nf, hd):
    nb, s, d = x.shape
    mrows = sc.shape[1]
    ts = min(s, 512)
    nfp = wft.shape[0]
    tri = (jnp.arange(ts)[:, None] <= jnp.arange(ts)[None, :]).astype(BF16)
    tok = lambda b, i: (b, i, 0)
    mod = lambda b, i: (b, 0, 0)
    const = lambda b, i: (0, 0)
    kern = functools.partial(_inproj_kernel, mix=mix, nf=nf, q_scale=float(hd) ** -0.5)
    return pl.pallas_call(
        kern,
        out_shape=(jax.ShapeDtypeStruct((nb, s, mix), BF16),
                   jax.ShapeDtypeStruct((nb, s, mix), F32),
                   jax.ShapeDtypeStruct((nb, s, mix), F32),
                   jax.ShapeDtypeStruct((nb, s, mix), BF16),
                   jax.ShapeDtypeStruct((nb, s, mix), BF16),
                   jax.ShapeDtypeStruct((nb, s, nf), F32),
                   jax.ShapeDtypeStruct((nb, nfp, s), F32)),
        grid=(nb, s // ts),
        in_specs=[pl.BlockSpec((1, ts, d), tok),
                  pl.BlockSpec((1, mrows, d), mod),
                  pl.BlockSpec((1, mrows, d), mod),
                  pl.BlockSpec((d, 3 * mix), const),
                  pl.BlockSpec((d, V7X_LANES), const),
                  pl.BlockSpec((nfp, d), const),
                  pl.BlockSpec((1, nf), const),
                  pl.BlockSpec((nfp, 1), const),
                  pl.BlockSpec((ts, ts), const)],
        out_specs=(pl.BlockSpec((1, ts, mix), tok),
                   pl.BlockSpec((1, ts, mix), tok),
                   pl.BlockSpec((1, ts, mix), tok),
                   pl.BlockSpec((1, ts, mix), tok),
                   pl.BlockSpec((1, ts, mix), tok),
                   pl.BlockSpec((1, ts, nf), tok),
                   pl.BlockSpec((1, nfp, ts), lambda b, i: (b, 0, i))),
        scratch_shapes=[pltpu.VMEM((nfp, V7X_LANES), F32)],
        compiler_params=pltpu.CompilerParams(
            dimension_semantics=("arbitrary", "arbitrary"),
            vmem_limit_bytes=V7X_VMEM_BUDGET),
        name="inproj",
    )(x, sc, sh, w_qkv, wf, wft, bf, bft, tri)


def _pair_rmsnorm(outs, lane, hd, g):
    first = (lane // hd) == 0
    o = jnp.where(first, outs[0], outs[1])
    osq = o * o
    ms0 = jnp.sum(jnp.where(first, osq, 0.0), axis=-1, keepdims=True)
    ms1 = jnp.sum(jnp.where(first, 0.0, osq), axis=-1, keepdims=True)
    ms = jnp.where(first, ms0, ms1) * (1.0 / hd)
    return o * lax.rsqrt(ms + RMS_EPS) * g


def _fox_kernel(q_ref, k_ref, v_ref, cum_ref, g_ref, o_ref, *, tq, hd):
    qi = pl.program_id(2)
    q = q_ref[0]
    lane = lax.broadcasted_iota(jnp.int32, (1, V7X_LANES), 1)
    row = lax.broadcasted_iota(jnp.int32, (tq, tq), 0)
    col = lax.broadcasted_iota(jnp.int32, (tq, tq), 1)
    outs = []
    for h in range(2):
        qh = jnp.where((lane // hd) == h, q, jnp.zeros_like(q))

        def step(kb, carry, masked, h=h, qh=qh):
            m, l, acc = carry
            k0 = pl.multiple_of(kb * tq, tq)
            s = _dot_nt(qh, k_ref[0, pl.ds(k0, tq), :])
            s = s - cum_ref[0, 0, h:h + 1, pl.ds(k0, tq)]
            if masked:
                s = jnp.where(col <= row, s, -jnp.inf)
            m_new = jnp.maximum(m, jnp.max(s, axis=-1, keepdims=True))
            alpha = jnp.exp(m - m_new)
            p = jnp.exp(s - m_new)
            l = alpha * l + jnp.sum(p, axis=-1, keepdims=True)
            acc = alpha * acc + _dot(p.astype(BF16), v_ref[0, pl.ds(k0, tq), :])
            return m_new, l, acc

        init = (jnp.full((tq, 1), -jnp.inf, F32), jnp.zeros((tq, 1), F32),
                jnp.zeros((tq, V7X_LANES), F32))
        carry = lax.fori_loop(0, qi, functools.partial(step, masked=False), init)
        _, l, acc = step(qi, carry, True)
        outs.append(acc / l)
    o_ref[0] = _pair_rmsnorm(outs, lane, hd, g_ref[...]).astype(BF16)


def _sb_kernel(q_ref, k_ref, v_ref, tri_ref, g_ref, o_ref, *, tq, hd):
    qi = pl.program_id(2)
    q = q_ref[0]
    lane = lax.broadcasted_iota(jnp.int32, (1, V7X_LANES), 1)
    row = lax.broadcasted_iota(jnp.int32, (tq, tq), 0)
    col = lax.broadcasted_iota(jnp.int32, (tq, tq), 1)
    outs = []
    for h in range(2):
        qh = jnp.where((lane // hd) == h, q, jnp.zeros_like(q))

        def step(kb, carry, masked, qh=qh):
            rsum, acc = carry
            k0 = pl.multiple_of(kb * tq, tq)
            z = _dot_nt(qh, k_ref[0, pl.ds(k0, tq), :])
            t = _softplus_neg_abs(z)
            lsz = jnp.minimum(z, 0.0) - t
            l1m = -jnp.maximum(z, 0.0) - t
            if masked:
                valid = col < row
                l1m = jnp.where(valid, l1m, 0.0)
            hi, lo = _split2(l1m)
            tri = tri_ref[...]
            after = _dot(hi, tri) + _dot(lo, tri)
            a = jnp.exp(lsz + after + rsum)
            if masked:
                a = jnp.where(valid, a, 0.0)
            rsum = rsum + jnp.sum(l1m, axis=-1, keepdims=True)
            acc = acc + _dot(a.astype(BF16), v_ref[0, pl.ds(k0, tq), :])
            return rsum, acc

        init = (jnp.zeros((tq, 1), F32), jnp.zeros((tq, V7X_LANES), F32))
        carry = step(qi, init, True)
        _, acc = lax.fori_loop(
            0, qi, lambda i, c, step=step: step(qi - 1 - i, c, False), carry)
        outs.append(acc)
    o_ref[0] = _pair_rmsnorm(outs, lane, hd, g_ref[...]).astype(BF16)


def _prompt_attention(q, kb, vb, cum4, g_mix2, *, nf, hd):
    b, s, mix = q.shape
    nh = mix // hd
    fox_pairs = nf // 2
    sb_pairs = (nh - nf) // 2
    tq = min(s, 256)
    nq = s // tq
    tri = (jnp.arange(tq)[:, None] > jnp.arange(tq)[None, :]).astype(BF16)
    params = pltpu.CompilerParams(
        dimension_semantics=("arbitrary", "arbitrary", "arbitrary"),
        vmem_limit_bytes=V7X_VMEM_BUDGET)

    def specs(off):
        return (pl.BlockSpec((1, tq, V7X_LANES), lambda bi, p, i: (bi, i, p + off)),
                pl.BlockSpec((1, s, V7X_LANES), lambda bi, p, i: (bi, 0, p + off)),
                pl.BlockSpec((1, s, V7X_LANES), lambda bi, p, i: (bi, 0, p + off)),
                pl.BlockSpec((1, V7X_LANES), lambda bi, p, i: (0, p + off)),
                pl.BlockSpec((1, tq, V7X_LANES), lambda bi, p, i: (bi, i, p)))

    qs, ks, vs, gs, os_ = specs(0)
    o_fox = pl.pallas_call(
        functools.partial(_fox_kernel, tq=tq, hd=hd),
        out_shape=jax.ShapeDtypeStruct((b, s, fox_pairs * V7X_LANES), BF16),
        grid=(b, fox_pairs, nq),
        in_specs=[qs, ks, vs,
                  pl.BlockSpec((1, 1, 2, s), lambda bi, p, i: (bi, p, 0, 0)),
                  gs],
        out_specs=os_,
        compiler_params=params,
        name="fox_prompt",
    )(q, kb, vb, cum4, g_mix2)
    qs, ks, vs, gs, os_ = specs(fox_pairs)
    o_sb = pl.pallas_call(
        functools.partial(_sb_kernel, tq=tq, hd=hd),
        out_shape=jax.ShapeDtypeStruct((b, s, sb_pairs * V7X_LANES), BF16),
        grid=(b, sb_pairs, nq),
        in_specs=[qs, ks, vs, pl.BlockSpec((tq, tq), lambda bi, p, i: (0, 0)), gs],
        out_specs=os_,
        compiler_params=params,
        name="sb_prompt",
    )(q, kb, vb, tri, g_mix2)
    return jnp.concatenate([o_fox, o_sb], axis=-1)


def _decode_kernel(pt_ref, q_ref, kn_ref, vn_ref, lfn_ref, tri_ref, g_ref, *rest,
                   pps, n_tok, nh, nf, hd):
    del pt_ref
    k_refs = rest[0:pps]
    v_refs = rest[pps:2 * pps]
    lf_refs = rest[2 * pps:3 * pps]
    o_ref = rest[3 * pps]
    qbd_ref, m_ref, l_ref, r_ref, gc_ref, acc_ref, kpad_ref, vpad_ref = rest[3 * pps + 1:]
    j = pl.program_id(1)
    rows = n_tok * nh
    ps = tri_ref.shape[0]
    mix = nh * hd
    rowi = lax.broadcasted_iota(jnp.int32, (rows, 1), 0)
    row_h = rowi % nh
    row_t = rowi // nh
    is_fox = row_h < nf
    key = lax.broadcasted_iota(jnp.int32, (1, ps), 1)
    lane = lax.broadcasted_iota(jnp.int32, (1, mix), 1)

    def process(kp, vp, lf_t, new):
        s = _dot_nt(qbd_ref[...], kp)
        tri = tri_ref[...]
        lf = lf_t[0:nf]
        hi, mid, lo = _split3(lf)
        g_suffix = _dot(hi, tri) + _dot(mid, tri) + _dot(lo, tri) + gc_ref[0:nf, 0:1]
        gc_ref[0:nf, :] = gc_ref[0:nf, :] + jnp.sum(lf, axis=-1, keepdims=True)
        if nh > nf:
            g_heads = jnp.concatenate([g_suffix, jnp.zeros((nh - nf, ps), F32)], axis=0)
        else:
            g_heads = g_suffix
        x = s + jnp.concatenate([g_heads] * n_tok, axis=0)
        if new:
            x = jnp.where((key < n_tok) & (key <= row_t), x, -jnp.inf)
        m_prev = m_ref[:, 0:1]
        m_new = jnp.maximum(m_prev, jnp.max(x, axis=-1, keepdims=True))
        alpha = jnp.exp(m_prev - m_new)
        p_fox = jnp.exp(x - m_new)
        l_new = alpha * l_ref[:, 0:1] + jnp.sum(p_fox, axis=-1, keepdims=True)
        m_ref[...] = jnp.broadcast_to(m_new, m_ref.shape)
        l_ref[...] = jnp.broadcast_to(l_new, l_ref.shape)
        t = _softplus_neg_abs(s)
        lsz = jnp.minimum(s, 0.0) - t
        l1m = -jnp.maximum(s, 0.0) - t
        if new:
            valid = key < row_t
            l1m = jnp.where(valid, l1m, 0.0)
        l_hi, l_lo = _split2(l1m)
        after = _dot(l_hi, tri) + _dot(l_lo, tri)
        r_prev = r_ref[:, 0:1]
        a_sb = jnp.exp(lsz + after + r_prev)
        if new:
            a_sb = jnp.where(valid, a_sb, 0.0)
        r_ref[...] = jnp.broadcast_to(r_prev + jnp.sum(l1m, axis=-1, keepdims=True), r_ref.shape)
        p = jnp.where(is_fox, p_fox, a_sb).astype(BF16)
        scale = jnp.where(is_fox, alpha, 1.0)
        acc_ref[...] = acc_ref[...] * scale + _dot(p, vp)

    @pl.when(j == 0)
    def _():
        q = q_ref[0]
        head_rows = lax.broadcasted_iota(jnp.int32, (nh, 1), 0)
        own = (lane // hd) == head_rows
        blocks = [jnp.where(own, jnp.broadcast_to(q[t:t + 1], (nh, mix)), 0.0)
                  for t in range(n_tok)]
        qbd_ref[...] = jnp.concatenate(blocks, axis=0).astype(BF16)
        m_ref[...] = jnp.full(m_ref.shape, -jnp.inf, F32)
        l_ref[...] = jnp.zeros_like(l_ref)
        r_ref[...] = jnp.zeros_like(r_ref)
        gc_ref[...] = jnp.zeros_like(gc_ref)
        acc_ref[...] = jnp.zeros_like(acc_ref)
        kpad_ref[...] = jnp.zeros_like(kpad_ref)
        vpad_ref[...] = jnp.zeros_like(vpad_ref)
        nrow = kn_ref.shape[1]
        kpad_ref[0:nrow, :] = kn_ref[0]
        vpad_ref[0:nrow, :] = vn_ref[0]
        process(kpad_ref[...], vpad_ref[...], lfn_ref[0], True)

    for i in range(pps):
        process(k_refs[i][0].astype(BF16), v_refs[i][0].astype(BF16), lf_refs[i][0, 0], False)

    @pl.when(j == pl.num_programs(1) - 1)
    def _():
        o = acc_ref[...] * jnp.where(is_fox, 1.0 / l_ref[:, 0:1], 1.0)
        own = (lane // hd) == row_h
        ms = jnp.sum(jnp.where(own, o * o, 0.0), axis=-1, keepdims=True) * (1.0 / hd)
        on = jnp.where(own, o * lax.rsqrt(ms + RMS_EPS), 0.0)
        outs = [jnp.sum(on[t * nh:(t + 1) * nh], axis=0, keepdims=True) for t in range(n_tok)]
        o_ref[0] = (jnp.concatenate(outs, axis=0) * g_ref[...]).astype(BF16)


def _decode_attention(q, k_new, v_new, lf_new_t, cache_k3, cache_v3, cache_lf_t, page_table, g_mix2,
                      *, layer, nh, nf, hd, ps):
    db, n_tok, mix = q.shape
    n_pages = page_table.shape[1]
    nfp = cache_lf_t.shape[2]
    pps = 4 if n_pages % 4 == 0 else (2 if n_pages % 2 == 0 else 1)
    rows = n_tok * nh
    tri = (jnp.arange(ps)[:, None] > jnp.arange(ps)[None, :]).astype(BF16)
    per_b = lambda b, j, pt: (b, 0, 0)
    const = lambda b, j, pt: (0, 0)

    def page_map(i):
        return lambda b, j, pt: (pt[b, n_pages - 1 - (j * pps + i)], layer, 0)

    def lf_map(i):
        return lambda b, j, pt: (pt[b, n_pages - 1 - (j * pps + i)], layer, 0, 0)

    in_specs = [pl.BlockSpec((1, n_tok, mix), per_b),
                pl.BlockSpec((1, k_new.shape[1], mix), per_b),
                pl.BlockSpec((1, v_new.shape[1], mix), per_b),
                pl.BlockSpec((1, nfp, ps), per_b),
                pl.BlockSpec((ps, ps), const),
                pl.BlockSpec((1, mix), const)]
    in_specs += [pl.BlockSpec((1, ps, mix), page_map(i)) for i in range(pps)]
    in_specs += [pl.BlockSpec((1, ps, mix), page_map(i)) for i in range(pps)]
    in_specs += [pl.BlockSpec((1, 1, nfp, ps), lf_map(i)) for i in range(pps)]
    grid_spec = pltpu.PrefetchScalarGridSpec(
        num_scalar_prefetch=1,
        grid=(db, n_pages // pps),
        in_specs=in_specs,
        out_specs=pl.BlockSpec((1, n_tok, mix), per_b),
        scratch_shapes=[pltpu.VMEM((rows, mix), BF16),
                        pltpu.VMEM((rows, V7X_LANES), F32),
                        pltpu.VMEM((rows, V7X_LANES), F32),
                        pltpu.VMEM((rows, V7X_LANES), F32),
                        pltpu.VMEM((nfp, V7X_LANES), F32),
                        pltpu.VMEM((rows, mix), F32),
                        pltpu.VMEM((ps, mix), BF16),
                        pltpu.VMEM((ps, mix), BF16)])
    return pl.pallas_call(
        functools.partial(_decode_kernel, pps=pps, n_tok=n_tok, nh=nh, nf=nf, hd=hd),
        out_shape=jax.ShapeDtypeStruct((db, n_tok, mix), BF16),
        grid_spec=grid_spec,
        compiler_params=pltpu.CompilerParams(
            dimension_semantics=("arbitrary", "arbitrary"),
            vmem_limit_bytes=V7X_VMEM_BUDGET),
        name="decode_attn",
    )(page_table, q, k_new, v_new, lf_new_t, tri, g_mix2,
      *([cache_k3] * pps), *([cache_v3] * pps), *([cache_lf_t] * pps))


def _outproj_kernel(mix_ref, x_ref, g1_ref, sc_ref, sh_ref, w_ref, lg_ref, lb_ref, wr_ref, br_ref,
                    x1_ref, h2_ref, gate_ref, *, alpha, ng, epg):
    mo = _dot(mix_ref[0], w_ref[...])
    x1 = _layer_norm(alpha * x_ref[0] + g1_ref[0] * mo, lg_ref[...], lb_ref[...])
    x1_ref[0] = x1
    h2 = x1 * (1.0 + sc_ref[0]) + sh_ref[0]
    h2_ref[0] = h2.astype(BF16)
    logits = jnp.dot(h2, wr_ref[...], precision=lax.Precision.HIGHEST,
                     preferred_element_type=F32) + br_ref[...]
    lane = lax.broadcasted_iota(jnp.int32, logits.shape, 1)
    big = jnp.int32(2 ** 30)
    neg = -jnp.inf
    gl = jnp.where(lane < ng, logits, neg)
    gmax = jnp.max(gl, axis=-1, keepdims=True)
    gidx = jnp.min(jnp.where(gl == gmax, lane, big), axis=-1, keepdims=True)
    g_w = 1.0 / jnp.sum(jnp.exp(gl - gmax), axis=-1, keepdims=True)
    in_group = (lane >= ng) & (lane < ng + (gidx + 1) * epg) & (lane >= ng + gidx * epg)
    el = jnp.where(in_group, logits, neg)
    m1 = jnp.max(el, axis=-1, keepdims=True)
    i1 = jnp.min(jnp.where(el == m1, lane, big), axis=-1, keepdims=True)
    el2 = jnp.where(lane == i1, neg, el)
    m2 = jnp.max(el2, axis=-1, keepdims=True)
    i2 = jnp.min(jnp.where(el2 == m2, lane, big), axis=-1, keepdims=True)
    den = jnp.sum(jnp.exp(el - m1), axis=-1, keepdims=True)
    p1 = 1.0 / den
    p2 = jnp.exp(m2 - m1) / den
    w1 = p1 / (p1 + p2)
    w2 = p2 / (p1 + p2)
    gate_ref[0] = jnp.where(lane == i1, g_w * w1, jnp.where(lane == i2, g_w * w2, 0.0))


def _outproj(mixn, x, g1, sc2, sh2, w_out, ln_g, ln_b, w_r, b_r, *, alpha, ng, epg):
    nb, s, d = x.shape
    mix = mixn.shape[2]
    mrows = g1.shape[1]
    ts = min(s, 512)
    tok = lambda b, i: (b, i, 0)
    mod = lambda b, i: (b, 0, 0)
    const = lambda b, i: (0, 0)
    return pl.pallas_call(
        functools.partial(_outproj_kernel, alpha=alpha, ng=ng, epg=epg),
        out_shape=(jax.ShapeDtypeStruct((nb, s, d), F32),
                   jax.ShapeDtypeStruct((nb, s, d), BF16),
                   jax.ShapeDtypeStruct((nb, s, V7X_LANES), F32)),
        grid=(nb, s // ts),
        in_specs=[pl.BlockSpec((1, ts, mix), tok),
                  pl.BlockSpec((1, ts, d), tok),
                  pl.BlockSpec((1, mrows, d), mod),
                  pl.BlockSpec((1, mrows, d), mod),
                  pl.BlockSpec((1, mrows, d), mod),
                  pl.BlockSpec((mix, d), const),
                  pl.BlockSpec((1, d), const),
                  pl.BlockSpec((1, d), const),
                  pl.BlockSpec((d, V7X_LANES), const),
                  pl.BlockSpec((1, V7X_LANES), const)],
        out_specs=(pl.BlockSpec((1, ts, d), tok),
                   pl.BlockSpec((1, ts, d), tok),
                   pl.BlockSpec((1, ts, V7X_LANES), tok)),
        compiler_params=pltpu.CompilerParams(
            dimension_semantics=("arbitrary", "arbitrary"),
            vmem_limit_bytes=V7X_VMEM_BUDGET),
        name="outproj_router",
    )(mixn, x, g1, sc2, sh2, w_out, ln_g, ln_b, w_r, b_r)


def _moe_kernel(h_ref, gate_ref, x1_ref, g2_ref, wg_ref, wu_ref, wd_ref, lg_ref, lb_ref,
                y_ref, acc_ref, *, alpha, ng, ep, f):
    e = pl.program_id(2)

    @pl.when(e == 0)
    def _():
        acc_ref[...] = jnp.zeros_like(acc_ref)

    h = h_ref[0]
    gates = gate_ref[0]
    lane = lax.broadcasted_iota(jnp.int32, gates.shape, 1)
    zs = []
    for i in range(ep):
        a = _dot(h, wg_ref[i])
        u = _dot(h, wu_ref[i])
        gcol = jnp.sum(jnp.where(lane == ng + e * ep + i, gates, 0.0), axis=-1, keepdims=True)
        zs.append((a * (1.0 / (1.0 + jnp.exp(-a))) * u * gcol).astype(BF16))
    zz = zs[0] if ep == 1 else jnp.concatenate(zs, axis=-1)
    acc_ref[...] += _dot(zz, wd_ref[...])

    @pl.when(e == pl.num_programs(2) - 1)
    def _():
        y_ref[0] = _layer_norm(alpha * x1_ref[0] + g2_ref[0] * acc_ref[...],
                               lg_ref[...], lb_ref[...])


def _moe(h2, gates, x1, g2, w_gate, w_up, w_down2, ln_g, ln_b, *, alpha, ng):
    nb, s, d = x1.shape
    ne, _, f = w_gate.shape
    mrows = g2.shape[1]
    tm = min(s, 1024)
    ep = 4 if ne % 4 == 0 else 1
    tok = lambda b, i, e: (b, i, 0)
    mod = lambda b, i, e: (b, 0, 0)
    const = lambda b, i, e: (0, 0)
    return pl.pallas_call(
        functools.partial(_moe_kernel, alpha=alpha, ng=ng, ep=ep, f=f),
        out_shape=jax.ShapeDtypeStruct((nb, s, d), F32),
        grid=(nb, s // tm, ne // ep),
        in_specs=[pl.BlockSpec((1, tm, d), tok),
                  pl.BlockSpec((1, tm, V7X_LANES), tok),
                  pl.BlockSpec((1, tm, d), tok),
                  pl.BlockSpec((1, mrows, d), mod),
                  pl.BlockSpec((ep, d, f), lambda b, i, e: (e, 0, 0)),
                  pl.BlockSpec((ep, d, f), lambda b, i, e: (e, 0, 0)),
                  pl.BlockSpec((ep * f, d), lambda b, i, e: (e, 0)),
                  pl.BlockSpec((1, d), const),
                  pl.BlockSpec((1, d), const)],
        out_specs=pl.BlockSpec((1, tm, d), tok),
        scratch_shapes=[pltpu.VMEM((tm, d), F32)],
        compiler_params=pltpu.CompilerParams(
            dimension_semantics=("arbitrary", "arbitrary", "arbitrary"),
            vmem_limit_bytes=V7X_VMEM_BUDGET),
        name="moe_experts",
    )(h2, gates, x1, g2, w_gate, w_up, w_down2, ln_g, ln_b)


def _pad_to(a, axis, size):
    pad = [(0, 0)] * a.ndim
    pad[axis] = (0, size - a.shape[axis])
    return jnp.pad(a, pad)


def kernel(x_prompt, x_sample, c_prompt, c_sample, cache_k, cache_v, cache_logf, page_table, w_in, b_forget, g_mix, w_out, w_ada, b_ada, ln1_g, ln1_b, ln2_g, ln2_b, w_router_group, b_router_group, w_router_expert, b_router_expert, w_gate_e, w_up_e, w_down_e):
    b, s, d = x_prompt.shape
    db, n_tok, _ = x_sample.shape
    depth = w_in.shape[0]
    n_pool, _, ps, nh, hd = cache_k.shape
    nf = b_forget.shape[1]
    mix = nh * hd
    ng = w_router_group.shape[2]
    ne = w_router_expert.shape[2]
    epg = ne // ng
    f = w_gate_e.shape[3]
    nfp = _round_up(nf, V7X_SUBLANES)
    alpha = float((2 * depth) ** 0.25)
    assert nf % 2 == 0 and (nh - nf) % 2 == 0 and 2 * hd == V7X_LANES
    assert ng + ne <= V7X_LANES and n_tok <= V7X_BF16_SUBLANES

    cache_k3 = cache_k.reshape(n_pool, depth * ps, mix)
    cache_v3 = cache_v.reshape(n_pool, depth * ps, mix)
    cache_lf_t = _pad_to(jnp.swapaxes(cache_logf, 2, 3), 2, nfp)

    n_c = b + db
    c_all = _pad_to(jnp.concatenate([c_prompt, c_sample], axis=0), 0, _round_up(n_c, V7X_SUBLANES))

    yp, ys = x_prompt, x_sample.reshape(1, db * n_tok, d)
    outs = {name: [] for name in ("kp", "vp", "lp", "ks", "vs", "ls")}
    for l in range(depth):
        mods = _adaln(c_all, w_ada[l], b_ada[l])
        mods_p = [m.reshape(b, 1, d) for m in jnp.split(mods[:b], 6, axis=-1)]
        mods_s = [jnp.repeat(m, n_tok, axis=0).reshape(1, db * n_tok, d)
                  for m in jnp.split(mods[b:n_c], 6, axis=-1)]

        w_qkv = w_in[l][:, :3 * mix].astype(BF16)
        w_f = w_in[l][:, 3 * mix:]
        wf = _pad_to(w_f, 1, V7X_LANES).astype(BF16)
        wft = _pad_to(w_f.T, 0, nfp).astype(BF16)
        bf = b_forget[l].reshape(1, nf)
        bft = _pad_to(b_forget[l].reshape(nf, 1), 0, nfp)
        g_mix2 = g_mix[l].reshape(1, mix)
        w_out_b = w_out[l].astype(BF16)
        w_r = _pad_to(jnp.concatenate([w_router_group[l], w_router_expert[l]], axis=1), 1, V7X_LANES)
        b_r = _pad_to(jnp.concatenate([b_router_group[l], b_router_expert[l]]).reshape(1, ng + ne),
                      1, V7X_LANES)
        wg_b = w_gate_e[l].astype(BF16)
        wu_b = w_up_e[l].astype(BF16)
        wd_b = w_down_e[l].astype(BF16).reshape(ne * f, d)
        lg1, lb1 = ln1_g[l].reshape(1, d), ln1_b[l].reshape(1, d)
        lg2, lb2 = ln2_g[l].reshape(1, d), ln2_b[l].reshape(1, d)
        proj = functools.partial(_inproj, w_qkv=w_qkv, wf=wf, wft=wft, bf=bf, bft=bft,
                                 mix=mix, nf=nf, hd=hd)
        ffn = functools.partial(_moe, w_gate=wg_b, w_up=wu_b, w_down2=wd_b, ln_g=lg2, ln_b=lb2,
                                alpha=alpha, ng=ng)
        mixer_out = functools.partial(_outproj, w_out=w_out_b, ln_g=lg1, ln_b=lb1, w_r=w_r, b_r=b_r,
                                      alpha=alpha, ng=ng, epg=epg)

        sh1, sc1, g1, sh2, sc2, g2 = mods_p
        q, k, v, kb, vb, lf, cum = proj(yp, sc1, sh1)
        cum4 = cum[:, :nf].reshape(b, nf // 2, 2, s)
        mixn = _prompt_attention(q, kb, vb, cum4, g_mix2, nf=nf, hd=hd)
        x1, h2, gates = mixer_out(mixn, yp, g1, sc2, sh2)
        yp = ffn(h2, gates, x1, g2)
        outs["kp"].append(k.reshape(b, s, nh, hd))
        outs["vp"].append(v.reshape(b, s, nh, hd))
        outs["lp"].append(lf)

        sh1, sc1, g1, sh2, sc2, g2 = mods_s
        q, k, v, kb, vb, lf, _ = proj(ys, sc1, sh1)
        q_s = q.reshape(db, n_tok, mix).astype(F32)
        k_new = _pad_to(kb.reshape(db, n_tok, mix), 1, V7X_BF16_SUBLANES)
        v_new = _pad_to(vb.reshape(db, n_tok, mix), 1, V7X_BF16_SUBLANES)
        lf_new_t = _pad_to(_pad_to(jnp.swapaxes(lf.reshape(db, n_tok, nf), 1, 2), 1, nfp), 2, ps)
        mixn = _decode_attention(q_s, k_new, v_new, lf_new_t, cache_k3, cache_v3, cache_lf_t,
                                 page_table, g_mix2, layer=l, nh=nh, nf=nf, hd=hd, ps=ps)
        x1, h2, gates = mixer_out(mixn.reshape(1, db * n_tok, mix), ys, g1, sc2, sh2)
        ys = ffn(h2, gates, x1, g2)
        outs["ks"].append(k.reshape(db, n_tok, nh, hd))
        outs["vs"].append(v.reshape(db, n_tok, nh, hd))
        outs["ls"].append(lf.reshape(db, n_tok, nf))

    stack = lambda name: jnp.stack(outs[name], axis=1)
    return (yp, ys.reshape(db, n_tok, d), stack("kp"), stack("vp"), stack("lp"),
            stack("ks"), stack("vs"), stack("ls"))
```

```python
import functools

import jax
import jax.numpy as jnp
from jax import lax
from jax.experimental import pallas as pl
from jax.experimental.pallas import tpu as pltpu

F32 = jnp.float32
BF16 = jnp.bfloat16
LN_EPS = 1e-5
RMS_EPS = 1e-6
LOG2E = 1.4426950408889634

V7X_LANES = 128
V7X_SUBLANES = 8
V7X_MXU_DIM = 256
V7X_VMEM_BUDGET = 56 * 1024 * 1024

_NT = (((1,), (1,)), ((), ()))


def _dot(a, b):
    return jnp.dot(a, b, preferred_element_type=F32)


def _dot_nt(a, b):
    return lax.dot_general(a, b, _NT, preferred_element_type=F32)


def _round_up(n, m):
    return (n + m - 1) // m * m


def _split3(x):
    hi = x.astype(BF16)
    r1 = x - hi.astype(F32)
    mid = r1.astype(BF16)
    lo = (r1 - mid.astype(F32)).astype(BF16)
    return hi, mid, lo


def _split2(x):
    hi = x.astype(BF16)
    lo = (x - hi.astype(F32)).astype(BF16)
    return hi, lo


def _dot_parts(parts, tri_stack):
    width = tri_stack.shape[1]
    return _dot(jnp.concatenate(parts, axis=-1), tri_stack[0:len(parts) * width])


def _stack_rows(tri, n):
    return jnp.concatenate([tri] * n, axis=0)


def _softplus_neg_abs(z):
    return jnp.log(1.0 + jnp.exp2(jnp.abs(z) * (-LOG2E)))


def _layer_norm(x, g, b):
    mu = jnp.mean(x, axis=-1, keepdims=True)
    xc = x - mu
    var = jnp.mean(xc * xc, axis=-1, keepdims=True)
    return xc * lax.rsqrt(var + LN_EPS) * g + b


def _adaln_kernel(c_ref, w_ref, b_ref, o_ref):
    c = c_ref[...]
    sc = c * (1.0 / (1.0 + jnp.exp(-c)))
    o_ref[...] = jnp.dot(sc, w_ref[...], precision=lax.Precision.HIGHEST,
                         preferred_element_type=F32) + b_ref[...]


def _adaln(c_all, w_ada, b_ada):
    rows, d = c_all.shape
    n = w_ada.shape[1]
    tn = min(n, 512)
    return pl.pallas_call(
        _adaln_kernel,
        out_shape=jax.ShapeDtypeStruct((rows, n), F32),
        grid=(n // tn,),
        in_specs=[pl.BlockSpec((rows, d), lambda j: (0, 0)),
                  pl.BlockSpec((d, tn), lambda j: (0, j)),
                  pl.BlockSpec((1, tn), lambda j: (0, j))],
        out_specs=pl.BlockSpec((rows, tn), lambda j: (0, j)),
        compiler_params=pltpu.CompilerParams(dimension_semantics=("arbitrary",)),
        name="adaln",
    )(c_all, w_ada, b_ada.reshape(1, n))


def _inproj_kernel(x_ref, sc_ref, sh_ref, w_ref, wf_ref, wft_ref, bf_ref, bft_ref, tri_ref, qs_ref,
                   q_ref, k_ref, v_ref, kb_ref, vb_ref, lf_ref, cum_ref, carry_ref,
                   *, mix, nf):
    si = pl.program_id(1)
    h = (x_ref[0] * (1.0 + sc_ref[0]) + sh_ref[0]).astype(BF16)
    q_ref[0] = (_dot(h, w_ref[:, 0:mix]) * qs_ref[...]).astype(BF16)
    k = _dot(h, w_ref[:, mix:2 * mix])
    k_ref[0] = k
    kb_ref[0] = k.astype(BF16)
    v = _dot(h, w_ref[:, 2 * mix:3 * mix])
    v_ref[0] = v
    vb_ref[0] = v.astype(BF16)

    logit = _dot(h, wf_ref[...])[:, 0:nf] + bf_ref[...]
    lf_ref[0] = jnp.minimum(logit, 0.0) - _softplus_neg_abs(logit)

    logit_t = _dot_nt(wft_ref[...], h) + bft_ref[...]
    lf_t = jnp.minimum(logit_t, 0.0) - _softplus_neg_abs(logit_t)
    local = _dot_parts(_split3(lf_t), tri_ref[...])

    @pl.when(si == 0)
    def _():
        carry_ref[...] = jnp.zeros_like(carry_ref)

    cum = local + carry_ref[:, 0:1]
    cum_ref[0] = cum
    ts = cum.shape[1]
    carry_ref[...] = jnp.broadcast_to(cum[:, ts - 1:ts], carry_ref.shape)


def _inproj(x, sc, sh, w_qkv, wf, wft, bf, bft, *, mix, nf, hd):
    nb, s, d = x.shape
    mrows = sc.shape[1]
    ts = min(s, 512)
    assert s % ts == 0
    nfp = wft.shape[0]
    tri = _stack_rows((jnp.arange(ts)[:, None] <= jnp.arange(ts)[None, :]).astype(BF16), 3)
    col_head = jnp.arange(mix) // hd
    q_scale = (jnp.where(col_head < nf, LOG2E, 1.0) * float(hd) ** -0.5).astype(F32).reshape(1, mix)
    tok = lambda b, i: (b, i, 0)
    mod = lambda b, i: (b, 0, 0)
    const = lambda b, i: (0, 0)
    return pl.pallas_call(
        functools.partial(_inproj_kernel, mix=mix, nf=nf),
        out_shape=(jax.ShapeDtypeStruct((nb, s, mix), BF16),
                   jax.ShapeDtypeStruct((nb, s, mix), F32),
                   jax.ShapeDtypeStruct((nb, s, mix), F32),
                   jax.ShapeDtypeStruct((nb, s, mix), BF16),
                   jax.ShapeDtypeStruct((nb, s, mix), BF16),
                   jax.ShapeDtypeStruct((nb, s, nf), F32),
                   jax.ShapeDtypeStruct((nb, nfp, s), F32)),
        grid=(nb, s // ts),
        in_specs=[pl.BlockSpec((1, ts, d), tok),
                  pl.BlockSpec((1, mrows, d), mod),
                  pl.BlockSpec((1, mrows, d), mod),
                  pl.BlockSpec((d, 3 * mix), const),
                  pl.BlockSpec((d, V7X_LANES), const),
                  pl.BlockSpec((nfp, d), const),
                  pl.BlockSpec((1, nf), const),
                  pl.BlockSpec((nfp, 1), const),
                  pl.BlockSpec((3 * ts, ts), const),
                  pl.BlockSpec((1, mix), const)],
        out_specs=(pl.BlockSpec((1, ts, mix), tok),
                   pl.BlockSpec((1, ts, mix), tok),
                   pl.BlockSpec((1, ts, mix), tok),
                   pl.BlockSpec((1, ts, mix), tok),
                   pl.BlockSpec((1, ts, mix), tok),
                   pl.BlockSpec((1, ts, nf), tok),
                   pl.BlockSpec((1, nfp, ts), lambda b, i: (b, 0, i))),
        scratch_shapes=[pltpu.VMEM((nfp, V7X_LANES), F32)],
        compiler_params=pltpu.CompilerParams(
            dimension_semantics=("arbitrary", "arbitrary"),
            vmem_limit_bytes=V7X_VMEM_BUDGET),
        name="inproj",
    )(x, sc, sh, w_qkv, wf, wft, bf, bft, tri, q_scale)


def _pair_rmsnorm(outs, lane, hd, g):
    first = (lane // hd) == 0
    o = jnp.where(first, outs[0], outs[1])
    osq = o * o
    ms0 = jnp.sum(jnp.where(first, osq, 0.0), axis=-1, keepdims=True)
    ms1 = jnp.sum(jnp.where(first, 0.0, osq), axis=-1, keepdims=True)
    ms = jnp.where(first, ms0, ms1) * (1.0 / hd)
    return o * lax.rsqrt(ms + RMS_EPS) * g


def _fox_kernel(q_ref, k_ref, v_ref, cum_ref, g_ref, o_ref, *, tq, hd):
    qi = pl.program_id(2)
    q = q_ref[0]
    lane = lax.broadcasted_iota(jnp.int32, (1, V7X_LANES), 1)
    row = lax.broadcasted_iota(jnp.int32, (tq, tq), 0)
    col = lax.broadcasted_iota(jnp.int32, (tq, tq), 1)
    qh = [jnp.where((lane // hd) == h, q, jnp.zeros_like(q)) for h in range(2)]

    def step(kb, carry, masked):
        k0 = pl.multiple_of(kb * tq, tq)
        k = k_ref[0, pl.ds(k0, tq), :]
        v = v_ref[0, pl.ds(k0, tq), :]
        new = []
        for h in range(2):
            m, l, acc = carry[h]
            s = _dot_nt(qh[h], k) - cum_ref[0, 0, h:h + 1, pl.ds(k0, tq)] * LOG2E
            if masked:
                s = jnp.where(col <= row, s, -jnp.inf)
            m_new = jnp.maximum(m, jnp.max(s, axis=-1, keepdims=True))
            alpha = jnp.exp2(m - m_new)
            p = jnp.exp2(s - m_new)
            l = alpha * l + jnp.sum(p, axis=-1, keepdims=True)
            acc = alpha * acc + _dot(p.astype(BF16), v)
            new.append((m_new, l, acc))
        return tuple(new)

    init = tuple((jnp.full((tq, 1), -jnp.inf, F32), jnp.zeros((tq, 1), F32),
                  jnp.zeros((tq, V7X_LANES), F32)) for _ in range(2))
    carry = lax.fori_loop(0, qi, functools.partial(step, masked=False), init)
    carry = step(qi, carry, True)
    outs = [acc / l for (_, l, acc) in carry]
    o_ref[0] = _pair_rmsnorm(outs, lane, hd, g_ref[...]).astype(BF16)


def _sb_kernel(q_ref, k_ref, v_ref, tri_ref, g_ref, o_ref, *, tq, sub, hd):
    qi = pl.program_id(2)
    q = q_ref[0]
    lane = lax.broadcasted_iota(jnp.int32, (1, V7X_LANES), 1)
    row = lax.broadcasted_iota(jnp.int32, (tq, tq), 0)
    col = lax.broadcasted_iota(jnp.int32, (tq, tq), 1)
    qh = [jnp.where((lane // hd) == h, q, jnp.zeros_like(q)) for h in range(2)]
    n_sub = tq // sub

    def step(kb, carry, masked):
        k0 = pl.multiple_of(kb * tq, tq)
        k = k_ref[0, pl.ds(k0, tq), :]
        v = v_ref[0, pl.ds(k0, tq), :]
        tri = tri_ref[...]
        new = []
        for h in range(2):
            rsum, acc = carry[h]
            z = _dot_nt(qh[h], k)
            sp = jnp.maximum(z, 0.0) + _softplus_neg_abs(z)
            lsz = z - sp
            if masked:
                valid = col < row
                sp = jnp.where(valid, sp, 0.0)
            afters = []
            later = rsum
            for c in reversed(range(n_sub)):
                part = sp[:, c * sub:(c + 1) * sub]
                afters.append(_dot_parts(_split2(part), tri) + later)
                later = later + jnp.sum(part, axis=-1, keepdims=True)
            after = afters[0] if n_sub == 1 else jnp.concatenate(afters[::-1], axis=-1)
            a = jnp.exp(lsz - after)
            if masked:
                a = jnp.where(valid, a, 0.0)
            new.append((later, acc + _dot(a.astype(BF16), v)))
        return tuple(new)

    init = tuple((jnp.zeros((tq, 1), F32), jnp.zeros((tq, V7X_LANES), F32)) for _ in range(2))
    carry = step(qi, init, True)
    carry = lax.fori_loop(0, qi, lambda i, c: step(qi - 1 - i, c, False), carry)
    outs = [acc for (_, acc) in carry]
    o_ref[0] = _pair_rmsnorm(outs, lane, hd, g_ref[...]).astype(BF16)


def _prompt_attention(q, kb, vb, cum4, g_mix2, *, nf, hd):
    b, s, mix = q.shape
    nh = mix // hd
    fox_pairs = nf // 2
    sb_pairs = (nh - nf) // 2
    tq = min(s, 512)
    sub = min(tq, V7X_MXU_DIM)
    assert s % tq == 0 and tq % sub == 0
    nq = s // tq
    tri = _stack_rows((jnp.arange(sub)[:, None] > jnp.arange(sub)[None, :]).astype(BF16), 2)
    params = pltpu.CompilerParams(
        dimension_semantics=("arbitrary", "arbitrary", "arbitrary"),
        vmem_limit_bytes=V7X_VMEM_BUDGET)

    def specs(off):
        return (pl.BlockSpec((1, tq, V7X_LANES), lambda bi, p, i: (bi, i, p + off)),
                pl.BlockSpec((1, s, V7X_LANES), lambda bi, p, i: (bi, 0, p + off)),
                pl.BlockSpec((1, s, V7X_LANES), lambda bi, p, i: (bi, 0, p + off)),
                pl.BlockSpec((1, V7X_LANES), lambda bi, p, i: (0, p + off)),
                pl.BlockSpec((1, tq, V7X_LANES), lambda bi, p, i: (bi, i, p)))

    qs, ks, vs, gs, os_ = specs(0)
    o_fox = pl.pallas_call(
        functools.partial(_fox_kernel, tq=tq, hd=hd),
        out_shape=jax.ShapeDtypeStruct((b, s, fox_pairs * V7X_LANES), BF16),
        grid=(b, fox_pairs, nq),
        in_specs=[qs, ks, vs,
                  pl.BlockSpec((1, 1, 2, s), lambda bi, p, i: (bi, p, 0, 0)),
                  gs],
        out_specs=os_,
        compiler_params=params,
        name="fox_prompt",
    )(q, kb, vb, cum4, g_mix2)
    qs, ks, vs, gs, os_ = specs(fox_pairs)
    o_sb = pl.pallas_call(
        functools.partial(_sb_kernel, tq=tq, sub=sub, hd=hd),
        out_shape=jax.ShapeDtypeStruct((b, s, sb_pairs * V7X_LANES), BF16),
        grid=(b, sb_pairs, nq),
        in_specs=[qs, ks, vs, pl.BlockSpec((2 * sub, sub), lambda bi, p, i: (0, 0)), gs],
        out_specs=os_,
        compiler_params=params,
        name="sb_prompt",
    )(q, kb, vb, tri, g_mix2)
    return jnp.concatenate([o_fox, o_sb], axis=-1)


def _decode_kernel(pt_ref, q_ref, knt_ref, vnt_ref, lfn_ref, tri_ref, g_ref, *rest,
                   pps, n_tok, nh, nf, hd):
    del pt_ref
    k_refs = rest[0:pps]
    v_refs = rest[pps:2 * pps]
    lf_refs = rest[2 * pps:3 * pps]
    o_ref = rest[3 * pps]
    qbd_ref, m_ref, l_ref, r_ref, gc_ref, acc_ref = rest[3 * pps + 1:]
    j = pl.program_id(1)
    rows = n_tok * nh
    ps = tri_ref.shape[1]
    mix = nh * hd
    rowi = lax.broadcasted_iota(jnp.int32, (rows, 1), 0)
    row_h = rowi % nh
    row_t = rowi // nh
    is_fox = row_h < nf
    key = lax.broadcasted_iota(jnp.int32, (1, ps), 1)
    lane = lax.broadcasted_iota(jnp.int32, (1, mix), 1)

    def process(blocks, new):
        tri = tri_ref[...]
        qbd = qbd_ref[...]
        g_off = gc_ref[0:nf, 0:1]
        r_off = r_ref[:, 0:1]
        xs, a_sbs = [], []
        for k_t, _, lf_t in blocks:
            s = _dot(qbd, k_t)
            lf = lf_t[0:nf]
            g_suffix = (_dot_parts(_split3(lf), tri) + g_off) * LOG2E
            g_off = g_off + jnp.sum(lf, axis=-1, keepdims=True)
            if nh > nf:
                g_suffix = jnp.concatenate([g_suffix, jnp.zeros((nh - nf, ps), F32)], axis=0)
            x = s + jnp.concatenate([g_suffix] * n_tok, axis=0)
            if new:
                x = jnp.where((key < n_tok) & (key <= row_t), x, -jnp.inf)
            xs.append(x)
            sp = jnp.maximum(s, 0.0) + _softplus_neg_abs(s)
            lsz = s - sp
            if new:
                valid = key < row_t
                sp = jnp.where(valid, sp, 0.0)
            a_sb = jnp.exp(lsz - (_dot_parts(_split2(sp), tri) + r_off))
            if new:
                a_sb = jnp.where(valid, a_sb, 0.0)
            a_sbs.append(a_sb)
            r_off = r_off + jnp.sum(sp, axis=-1, keepdims=True)
        gc_ref[0:nf, :] = jnp.broadcast_to(g_off, (nf, gc_ref.shape[1]))
        r_ref[...] = jnp.broadcast_to(r_off, r_ref.shape)
        m_prev = m_ref[:, 0:1]
        m_new = m_prev
        for x in xs:
            m_new = jnp.maximum(m_new, jnp.max(x, axis=-1, keepdims=True))
        alpha = jnp.exp2(m_prev - m_new)
        l_new = alpha * l_ref[:, 0:1]
        pv = None
        for x, a_sb, (_, v_t, _) in zip(xs, a_sbs, blocks):
            p_fox = jnp.exp2(x - m_new)
            l_new = l_new + jnp.sum(p_fox, axis=-1, keepdims=True)
            d = _dot_nt(jnp.where(is_fox, p_fox, a_sb).astype(BF16), v_t)
            pv = d if pv is None else pv + d
        m_ref[...] = jnp.broadcast_to(m_new, m_ref.shape)
        l_ref[...] = jnp.broadcast_to(l_new, l_ref.shape)
        acc_ref[...] = acc_ref[...] * jnp.where(is_fox, alpha, 1.0) + pv

    @pl.when(j == 0)
    def _():
        q = q_ref[0]
        head_rows = lax.broadcasted_iota(jnp.int32, (nh, 1), 0)
        own = (lane // hd) == head_rows
        qbd_ref[...] = jnp.concatenate(
            [jnp.where(own, jnp.broadcast_to(q[t:t + 1], (nh, mix)), 0.0) for t in range(n_tok)],
            axis=0).astype(BF16)
        m_ref[...] = jnp.full(m_ref.shape, -jnp.inf, F32)
        l_ref[...] = jnp.zeros_like(l_ref)
        r_ref[...] = jnp.zeros_like(r_ref)
        gc_ref[...] = jnp.zeros_like(gc_ref)
        acc_ref[...] = jnp.zeros_like(acc_ref)
        process([(knt_ref[0], vnt_ref[0], lfn_ref[0])], True)

    process([(k_refs[i][0, 0].astype(BF16), v_refs[i][0, 0].astype(BF16), lf_refs[i][0, 0])
             for i in range(pps)], False)

    @pl.when(j == pl.num_programs(1) - 1)
    def _():
        o = acc_ref[...] * jnp.where(is_fox, 1.0 / l_ref[:, 0:1], 1.0)
        own = (lane // hd) == row_h
        ms = jnp.sum(jnp.where(own, o * o, 0.0), axis=-1, keepdims=True) * (1.0 / hd)
        on = jnp.where(own, o * lax.rsqrt(ms + RMS_EPS), 0.0)
        outs = [jnp.sum(on[t * nh:(t + 1) * nh], axis=0, keepdims=True) for t in range(n_tok)]
        o_ref[0] = (jnp.concatenate(outs, axis=0) * g_ref[...]).astype(BF16)


def _decode_attention(q, k_new_t, v_new_t, lf_new_t, cache_kt, cache_vt, cache_lf_t, page_table,
                      g_mix2, *, layer, nh, nf, hd):
    db, n_tok, mix = q.shape
    n_pages = page_table.shape[1]
    nfp, ps = cache_lf_t.shape[2:]
    pps = max(p for p in (8, 4, 2, 1) if n_pages % p == 0)
    rows = n_tok * nh
    tri = _stack_rows((jnp.arange(ps)[:, None] > jnp.arange(ps)[None, :]).astype(BF16), 3)
    per_b = lambda b, j, pt: (b, 0, 0)
    const = lambda b, j, pt: (0, 0)

    def page_map(i):
        return lambda b, j, pt: (pt[b, n_pages - 1 - (j * pps + i)], layer, 0, 0)

    in_specs = [pl.BlockSpec((1, n_tok, mix), per_b),
                pl.BlockSpec((1, mix, ps), per_b),
                pl.BlockSpec((1, mix, ps), per_b),
                pl.BlockSpec((1, nfp, ps), per_b),
                pl.BlockSpec((3 * ps, ps), const),
                pl.BlockSpec((1, mix), const)]
    in_specs += [pl.BlockSpec((1, 1, mix, ps), page_map(i)) for i in range(pps)]
    in_specs += [pl.BlockSpec((1, 1, mix, ps), page_map(i)) for i in range(pps)]
    in_specs += [pl.BlockSpec((1, 1, nfp, ps), page_map(i)) for i in range(pps)]
    grid_spec = pltpu.PrefetchScalarGridSpec(
        num_scalar_prefetch=1,
        grid=(db, n_pages // pps),
        in_specs=in_specs,
        out_specs=pl.BlockSpec((1, n_tok, mix), per_b),
        scratch_shapes=[pltpu.VMEM((rows, mix), BF16),
                        pltpu.VMEM((rows, V7X_LANES), F32),
                        pltpu.VMEM((rows, V7X_LANES), F32),
                        pltpu.VMEM((rows, V7X_LANES), F32),
                        pltpu.VMEM((nfp, V7X_LANES), F32),
                        pltpu.VMEM((rows, mix), F32)])
    return pl.pallas_call(
        functools.partial(_decode_kernel, pps=pps, n_tok=n_tok, nh=nh, nf=nf, hd=hd),
        out_shape=jax.ShapeDtypeStruct((db, n_tok, mix), BF16),
        grid_spec=grid_spec,
        compiler_params=pltpu.CompilerParams(
            dimension_semantics=("arbitrary", "arbitrary"),
            vmem_limit_bytes=V7X_VMEM_BUDGET),
        name="decode_attn",
    )(page_table, q, k_new_t, v_new_t, lf_new_t, tri, g_mix2,
      *([cache_kt] * pps), *([cache_vt] * pps), *([cache_lf_t] * pps))


def _outproj_kernel(mix_ref, x_ref, g1_ref, sc_ref, sh_ref, w_ref, lg_ref, lb_ref, wr_ref, br_ref,
                    x1_ref, h2_ref, gate_ref, *, alpha, ng, epg):
    mo = _dot(mix_ref[0], w_ref[...])
    x1 = _layer_norm(alpha * x_ref[0] + g1_ref[0] * mo, lg_ref[...], lb_ref[...])
    x1_ref[0] = x1
    h2 = x1 * (1.0 + sc_ref[0]) + sh_ref[0]
    h2_ref[0] = h2.astype(BF16)
    logits = jnp.dot(h2, wr_ref[...], precision=lax.Precision.HIGHEST,
                     preferred_element_type=F32) + br_ref[...]
    lane = lax.broadcasted_iota(jnp.int32, logits.shape, 1)
    big = jnp.int32(2 ** 30)
    neg = -jnp.inf
    gl = jnp.where(lane < ng, logits, neg)
    gmax = jnp.max(gl, axis=-1, keepdims=True)
    gidx = jnp.min(jnp.where(gl == gmax, lane, big), axis=-1, keepdims=True)
    g_w = 1.0 / jnp.sum(jnp.exp(gl - gmax), axis=-1, keepdims=True)
    in_group = (lane >= ng + gidx * epg) & (lane < ng + (gidx + 1) * epg)
    el = jnp.where(in_group, logits, neg)
    m1 = jnp.max(el, axis=-1, keepdims=True)
    i1 = jnp.min(jnp.where(el == m1, lane, big), axis=-1, keepdims=True)
    el2 = jnp.where(lane == i1, neg, el)
    m2 = jnp.max(el2, axis=-1, keepdims=True)
    i2 = jnp.min(jnp.where(el2 == m2, lane, big), axis=-1, keepdims=True)
    den = jnp.sum(jnp.exp(el - m1), axis=-1, keepdims=True)
    p1 = 1.0 / den
    p2 = jnp.exp(m2 - m1) / den
    w1 = p1 / (p1 + p2)
    w2 = p2 / (p1 + p2)
    gate_ref[0] = jnp.where(lane == i1, g_w * w1, jnp.where(lane == i2, g_w * w2, 0.0))


def _outproj(mixn, x, g1, sc2, sh2, w_out, ln_g, ln_b, w_r, b_r, *, alpha, ng, epg):
    nb, s, d = x.shape
    mix = mixn.shape[2]
    mrows = g1.shape[1]
    ts = min(s, 512)
    assert s % ts == 0
    tok = lambda b, i: (b, i, 0)
    mod = lambda b, i: (b, 0, 0)
    const = lambda b, i: (0, 0)
    return pl.pallas_call(
        functools.partial(_outproj_kernel, alpha=alpha, ng=ng, epg=epg),
        out_shape=(jax.ShapeDtypeStruct((nb, s, d), F32),
                   jax.ShapeDtypeStruct((nb, s, d), BF16),
                   jax.ShapeDtypeStruct((nb, s, V7X_LANES), F32)),
        grid=(nb, s // ts),
        in_specs=[pl.BlockSpec((1, ts, mix), tok),
                  pl.BlockSpec((1, ts, d), tok),
                  pl.BlockSpec((1, mrows, d), mod),
                  pl.BlockSpec((1, mrows, d), mod),
                  pl.BlockSpec((1, mrows, d), mod),
                  pl.BlockSpec((mix, d), const),
                  pl.BlockSpec((1, d), const),
                  pl.BlockSpec((1, d), const),
                  pl.BlockSpec((d, V7X_LANES), const),
                  pl.BlockSpec((1, V7X_LANES), const)],
        out_specs=(pl.BlockSpec((1, ts, d), tok),
                   pl.BlockSpec((1, ts, d), tok),
                   pl.BlockSpec((1, ts, V7X_LANES), tok)),
        compiler_params=pltpu.CompilerParams(
            dimension_semantics=("arbitrary", "arbitrary"),
            vmem_limit_bytes=V7X_VMEM_BUDGET),
        name="outproj_router",
    )(mixn, x, g1, sc2, sh2, w_out, ln_g, ln_b, w_r, b_r)


def _moe_kernel(h_ref, gate_ref, x1_ref, g2_ref, wg_ref, wu_ref, wd_ref, lg_ref, lb_ref,
                y_ref, acc_ref, *, alpha, ng, ep):
    e = pl.program_id(2)

    @pl.when(e == 0)
    def _():
        acc_ref[...] = jnp.zeros_like(acc_ref)

    h = h_ref[0]
    gates = gate_ref[0]
    lane = lax.broadcasted_iota(jnp.int32, gates.shape, 1)
    zs = []
    for i in range(ep):
        a = _dot(h, wg_ref[i])
        u = _dot(h, wu_ref[i])
        gcol = jnp.sum(jnp.where(lane == ng + e * ep + i, gates, 0.0), axis=-1, keepdims=True)
        zs.append((a * (1.0 / (1.0 + jnp.exp(-a))) * u * gcol).astype(BF16))
    zz = zs[0] if ep == 1 else jnp.concatenate(zs, axis=-1)
    acc_ref[...] += _dot(zz, wd_ref[...])

    @pl.when(e == pl.num_programs(2) - 1)
    def _():
        y_ref[0] = _layer_norm(alpha * x1_ref[0] + g2_ref[0] * acc_ref[...],
                               lg_ref[...], lb_ref[...])


def _moe(h2, gates, x1, g2, w_gate, w_up, w_down2, ln_g, ln_b, *, alpha, ng):
    nb, s, d = x1.shape
    ne, _, f = w_gate.shape
    mrows = g2.shape[1]
    tm = min(s, 1024)
    assert s % tm == 0
    ep = 4 if ne % 4 == 0 else 1
    tok = lambda b, i, e: (b, i, 0)
    mod = lambda b, i, e: (b, 0, 0)
    const = lambda b, i, e: (0, 0)
    return pl.pallas_call(
        functools.partial(_moe_kernel, alpha=alpha, ng=ng, ep=ep),
        out_shape=jax.ShapeDtypeStruct((nb, s, d), F32),
        grid=(nb, s // tm, ne // ep),
        in_specs=[pl.BlockSpec((1, tm, d), tok),
                  pl.BlockSpec((1, tm, V7X_LANES), tok),
                  pl.BlockSpec((1, tm, d), tok),
                  pl.BlockSpec((1, mrows, d), mod),
                  pl.BlockSpec((ep, d, f), lambda b, i, e: (e, 0, 0)),
                  pl.BlockSpec((ep, d, f), lambda b, i, e: (e, 0, 0)),
                  pl.BlockSpec((ep * f, d), lambda b, i, e: (e, 0)),
                  pl.BlockSpec((1, d), const),
                  pl.BlockSpec((1, d), const)],
        out_specs=pl.BlockSpec((1, tm, d), tok),
        scratch_shapes=[pltpu.VMEM((tm, d), F32)],
        compiler_params=pltpu.CompilerParams(
            dimension_semantics=("arbitrary", "arbitrary", "arbitrary"),
            vmem_limit_bytes=V7X_VMEM_BUDGET),
        name="moe_experts",
    )(h2, gates, x1, g2, w_gate, w_up, w_down2, ln_g, ln_b)


def _pad_to(a, axis, size):
    if a.shape[axis] == size:
        return a
    pad = [(0, 0)] * a.ndim
    pad[axis] = (0, size - a.shape[axis])
    return jnp.pad(a, pad)


def kernel(x_prompt, x_sample, c_prompt, c_sample, cache_k, cache_v, cache_logf, page_table, w_in, b_forget, g_mix, w_out, w_ada, b_ada, ln1_g, ln1_b, ln2_g, ln2_b, w_router_group, b_router_group, w_router_expert, b_router_expert, w_gate_e, w_up_e, w_down_e):
    b, s, d = x_prompt.shape
    db, n_tok, _ = x_sample.shape
    depth = w_in.shape[0]
    n_pool, _, ps, nh, hd = cache_k.shape
    nf = b_forget.shape[1]
    mix = nh * hd
    ng = w_router_group.shape[2]
    ne = w_router_expert.shape[2]
    epg = ne // ng
    f = w_gate_e.shape[3]
    nfp = _round_up(nf, V7X_SUBLANES)
    alpha = float((2 * depth) ** 0.25)
    assert nf % 2 == 0 and (nh - nf) % 2 == 0 and 2 * hd == V7X_LANES
    assert ng + ne <= V7X_LANES and n_tok <= ps

    cache_kt = jnp.transpose(cache_k, (0, 1, 3, 4, 2)).reshape(n_pool, depth, mix, ps)
    cache_vt = jnp.transpose(cache_v, (0, 1, 3, 4, 2)).reshape(n_pool, depth, mix, ps)
    cache_lf_t = _pad_to(jnp.swapaxes(cache_logf, 2, 3), 2, nfp)

    n_c = b + db
    c_all = _pad_to(jnp.concatenate([c_prompt, c_sample], axis=0), 0, _round_up(n_c, V7X_SUBLANES))

    yp, ys = x_prompt, x_sample.reshape(1, db * n_tok, d)
    outs = {name: [] for name in ("kp", "vp", "lp", "ks", "vs", "ls")}
    for l in range(depth):
        mods = _adaln(c_all, w_ada[l], b_ada[l])
        mods_p = [m.reshape(b, 1, d) for m in jnp.split(mods[:b], 6, axis=-1)]
        mods_s = [jnp.repeat(m, n_tok, axis=0).reshape(1, db * n_tok, d)
                  for m in jnp.split(mods[b:n_c], 6, axis=-1)]

        w_qkv = w_in[l][:, :3 * mix].astype(BF16)
        w_f = w_in[l][:, 3 * mix:]
        wf = _pad_to(w_f, 1, V7X_LANES).astype(BF16)
        wft = _pad_to(w_f.T, 0, nfp).astype(BF16)
        bf = b_forget[l].reshape(1, nf)
        bft = _pad_to(b_forget[l].reshape(nf, 1), 0, nfp)
        g_mix2 = g_mix[l].reshape(1, mix)
        w_out_b = w_out[l].astype(BF16)
        w_r = _pad_to(jnp.concatenate([w_router_group[l], w_router_expert[l]], axis=1), 1, V7X_LANES)
        b_r = _pad_to(jnp.concatenate([b_router_group[l], b_router_expert[l]]).reshape(1, ng + ne),
                      1, V7X_LANES)
        wg_b = w_gate_e[l].astype(BF16)
        wu_b = w_up_e[l].astype(BF16)
        wd_b = w_down_e[l].astype(BF16).reshape(ne * f, d)
        lg1, lb1 = ln1_g[l].reshape(1, d), ln1_b[l].reshape(1, d)
        lg2, lb2 = ln2_g[l].reshape(1, d), ln2_b[l].reshape(1, d)
        proj = functools.partial(_inproj, w_qkv=w_qkv, wf=wf, wft=wft, bf=bf, bft=bft,
                                 mix=mix, nf=nf, hd=hd)
        ffn = functools.partial(_moe, w_gate=wg_b, w_up=wu_b, w_down2=wd_b, ln_g=lg2, ln_b=lb2,
                                alpha=alpha, ng=ng)
        mixer_out = functools.partial(_outproj, w_out=w_out_b, ln_g=lg1, ln_b=lb1, w_r=w_r, b_r=b_r,
                                      alpha=alpha, ng=ng, epg=epg)

        sh1, sc1, g1, sh2, sc2, g2 = mods_p
        q, k, v, kb, vb, lf, cum = proj(yp, sc1, sh1)
        cum4 = cum[:, :nf].reshape(b, nf // 2, 2, s)
        mixn = _prompt_attention(q, kb, vb, cum4, g_mix2, nf=nf, hd=hd)
        x1, h2, gates = mixer_out(mixn, yp, g1, sc2, sh2)
        yp = ffn(h2, gates, x1, g2)
        outs["kp"].append(k.reshape(b, s, nh, hd))
        outs["vp"].append(v.reshape(b, s, nh, hd))
        outs["lp"].append(lf)

        sh1, sc1, g1, sh2, sc2, g2 = mods_s
        q, k, v, kb, vb, lf, _ = proj(ys, sc1, sh1)
        q_s = q.reshape(db, n_tok, mix).astype(F32)
        k_new_t = _pad_to(jnp.swapaxes(kb.reshape(db, n_tok, mix), 1, 2), 2, ps)
        v_new_t = _pad_to(jnp.swapaxes(vb.reshape(db, n_tok, mix), 1, 2), 2, ps)
        lf_new_t = _pad_to(_pad_to(jnp.swapaxes(lf.reshape(db, n_tok, nf), 1, 2), 1, nfp), 2, ps)
        mixn = _decode_attention(q_s, k_new_t, v_new_t, lf_new_t, cache_kt, cache_vt, cache_lf_t,
                                 page_table, g_mix2, layer=l, nh=nh, nf=nf, hd=hd)
        x1, h2, gates = mixer_out(mixn.reshape(1, db * n_tok, mix), ys, g1, sc2, sh2)
        ys = ffn(h2, gates, x1, g2)
        outs["ks"].append(k.reshape(db, n_tok, nh, hd))
        outs["vs"].append(v.reshape(db, n_tok, nh, hd))
        outs["ls"].append(lf.reshape(db, n_tok, nf))

    stack = lambda name: jnp.stack(outs[name], axis=1)
    return (yp, ys.reshape(db, n_tok, d), stack("kp"), stack("vp"), stack("lp"),
            stack("ks"), stack("vs"), stack("ls"))
```

```python
import functools

import jax
import jax.numpy as jnp
from jax import lax
from jax.experimental import pallas as pl
from jax.experimental.pallas import tpu as pltpu

F32 = jnp.float32
BF16 = jnp.bfloat16
LN_EPS = 1e-5
RMS_EPS = 1e-6
LOG2E = 1.4426950408889634

V7X_LANES = 128
V7X_SUBLANES = 8
V7X_MXU_DIM = 256
V7X_VMEM_BUDGET = 56 * 1024 * 1024

_NT = (((1,), (1,)), ((), ()))


def _dot(a, b):
    return jnp.dot(a, b, preferred_element_type=F32)


def _dot_nt(a, b):
    return lax.dot_general(a, b, _NT, preferred_element_type=F32)


def _round_up(n, m):
    return (n + m - 1) // m * m


def _split3(x):
    hi = x.astype(BF16)
    r1 = x - hi.astype(F32)
    mid = r1.astype(BF16)
    lo = (r1 - mid.astype(F32)).astype(BF16)
    return hi, mid, lo


def _split2(x):
    hi = x.astype(BF16)
    lo = (x - hi.astype(F32)).astype(BF16)
    return hi, lo


def _dot_parts(parts, tri_stack):
    width = tri_stack.shape[1]
    return _dot(jnp.concatenate(parts, axis=-1), tri_stack[0:len(parts) * width])


def _stack_rows(tri, n):
    return jnp.concatenate([tri] * n, axis=0)


def _softplus_neg_abs(z):
    return jnp.log(1.0 + jnp.exp2(jnp.abs(z) * (-LOG2E)))


def _layer_norm(x, g, b):
    mu = jnp.mean(x, axis=-1, keepdims=True)
    xc = x - mu
    var = jnp.mean(xc * xc, axis=-1, keepdims=True)
    return xc * lax.rsqrt(var + LN_EPS) * g + b


def _adaln_kernel(c_ref, w_ref, b_ref, o_ref):
    c = c_ref[...]
    sc = c * (1.0 / (1.0 + jnp.exp(-c)))
    o_ref[...] = jnp.dot(sc, w_ref[...], precision=lax.Precision.HIGHEST,
                         preferred_element_type=F32) + b_ref[...]


def _adaln(c_all, w_ada, b_ada):
    rows, d = c_all.shape
    n = w_ada.shape[1]
    tn = min(n, 512)
    return pl.pallas_call(
        _adaln_kernel,
        out_shape=jax.ShapeDtypeStruct((rows, n), F32),
        grid=(n // tn,),
        in_specs=[pl.BlockSpec((rows, d), lambda j: (0, 0)),
                  pl.BlockSpec((d, tn), lambda j: (0, j)),
                  pl.BlockSpec((1, tn), lambda j: (0, j))],
        out_specs=pl.BlockSpec((rows, tn), lambda j: (0, j)),
        compiler_params=pltpu.CompilerParams(dimension_semantics=("arbitrary",)),
        name="adaln",
    )(c_all, w_ada, b_ada.reshape(1, n))


def _inproj_kernel(x_ref, sc_ref, sh_ref, w_ref, wf_ref, wft_ref, bf_ref, bft_ref, tri_ref, qs_ref,
                   q_ref, k_ref, v_ref, kb_ref, vb_ref, lf_ref, cum_ref, carry_ref,
                   *, mix, nf):
    si = pl.program_id(1)
    h = (x_ref[0] * (1.0 + sc_ref[0]) + sh_ref[0]).astype(BF16)
    q_ref[0] = (_dot(h, w_ref[:, 0:mix]) * qs_ref[...]).astype(BF16)
    k = _dot(h, w_ref[:, mix:2 * mix])
    k_ref[0] = k
    kb_ref[0] = k.astype(BF16)
    v = _dot(h, w_ref[:, 2 * mix:3 * mix])
    v_ref[0] = v
    vb_ref[0] = v.astype(BF16)

    logit = _dot(h, wf_ref[...])[:, 0:nf] + bf_ref[...]
    lf_ref[0] = jnp.minimum(logit, 0.0) - _softplus_neg_abs(logit)

    logit_t = _dot_nt(wft_ref[...], h) + bft_ref[...]
    lf_t = jnp.minimum(logit_t, 0.0) - _softplus_neg_abs(logit_t)
    local = _dot_parts(_split3(lf_t), tri_ref[...])

    @pl.when(si == 0)
    def _():
        carry_ref[...] = jnp.zeros_like(carry_ref)

    cum = local + carry_ref[:, 0:1]
    cum_ref[0] = cum
    ts = cum.shape[1]
    carry_ref[...] = jnp.broadcast_to(cum[:, ts - 1:ts], carry_ref.shape)


def _inproj(x, sc, sh, w_qkv, wf, wft, bf, bft, *, mix, nf, hd):
    nb, s, d = x.shape
    mrows = sc.shape[1]
    ts = min(s, 512)
    assert s % ts == 0
    nfp = wft.shape[0]
    tri = _stack_rows((jnp.arange(ts)[:, None] <= jnp.arange(ts)[None, :]).astype(BF16), 3)
    col_head = jnp.arange(mix) // hd
    q_scale = (jnp.where(col_head < nf, LOG2E, 1.0) * float(hd) ** -0.5).astype(F32).reshape(1, mix)
    tok = lambda b, i: (b, i, 0)
    mod = lambda b, i: (b, 0, 0)
    const = lambda b, i: (0, 0)
    return pl.pallas_call(
        functools.partial(_inproj_kernel, mix=mix, nf=nf),
        out_shape=(jax.ShapeDtypeStruct((nb, s, mix), BF16),
                   jax.ShapeDtypeStruct((nb, s, mix), F32),
                   jax.ShapeDtypeStruct((nb, s, mix), F32),
                   jax.ShapeDtypeStruct((nb, s, mix), BF16),
                   jax.ShapeDtypeStruct((nb, s, mix), BF16),
                   jax.ShapeDtypeStruct((nb, s, nf), F32),
                   jax.ShapeDtypeStruct((nb, nfp, s), F32)),
        grid=(nb, s // ts),
        in_specs=[pl.BlockSpec((1, ts, d), tok),
                  pl.BlockSpec((1, mrows, d), mod),
                  pl.BlockSpec((1, mrows, d), mod),
                  pl.BlockSpec((d, 3 * mix), const),
                  pl.BlockSpec((d, V7X_LANES), const),
                  pl.BlockSpec((nfp, d), const),
                  pl.BlockSpec((1, nf), const),
                  pl.BlockSpec((nfp, 1), const),
                  pl.BlockSpec((3 * ts, ts), const),
                  pl.BlockSpec((1, mix), const)],
        out_specs=(pl.BlockSpec((1, ts, mix), tok),
                   pl.BlockSpec((1, ts, mix), tok),
                   pl.BlockSpec((1, ts, mix), tok),
                   pl.BlockSpec((1, ts, mix), tok),
                   pl.BlockSpec((1, ts, mix), tok),
                   pl.BlockSpec((1, ts, nf), tok),
                   pl.BlockSpec((1, nfp, ts), lambda b, i: (b, 0, i))),
        scratch_shapes=[pltpu.VMEM((nfp, V7X_LANES), F32)],
        compiler_params=pltpu.CompilerParams(
            dimension_semantics=("arbitrary", "arbitrary"),
            vmem_limit_bytes=V7X_VMEM_BUDGET),
        name="inproj",
    )(x, sc, sh, w_qkv, wf, wft, bf, bft, tri, q_scale)


def _pair_rmsnorm(outs, lane, hd, g):
    first = (lane // hd) == 0
    o = jnp.where(first, outs[0], outs[1])
    osq = o * o
    ms0 = jnp.sum(jnp.where(first, osq, 0.0), axis=-1, keepdims=True)
    ms1 = jnp.sum(jnp.where(first, 0.0, osq), axis=-1, keepdims=True)
    ms = jnp.where(first, ms0, ms1) * (1.0 / hd)
    return o * lax.rsqrt(ms + RMS_EPS) * g


def _fox_kernel(q_ref, k_ref, v_ref, cum_ref, g_ref, o_ref, *, tq, hd):
    qi = pl.program_id(2)
    q = q_ref[0]
    lane = lax.broadcasted_iota(jnp.int32, (1, V7X_LANES), 1)
    row = lax.broadcasted_iota(jnp.int32, (tq, tq), 0)
    col = lax.broadcasted_iota(jnp.int32, (tq, tq), 1)
    qh = [jnp.where((lane // hd) == h, q, jnp.zeros_like(q)) for h in range(2)]

    def step(kb, carry, masked):
        k0 = pl.multiple_of(kb * tq, tq)
        k = k_ref[0, pl.ds(k0, tq), :]
        v = v_ref[0, pl.ds(k0, tq), :]
        new = []
        for h in range(2):
            m, l, acc = carry[h]
            s = _dot_nt(qh[h], k) - cum_ref[0, 0, h:h + 1, pl.ds(k0, tq)] * LOG2E
            if masked:
                s = jnp.where(col <= row, s, -jnp.inf)
            m_new = jnp.maximum(m, jnp.max(s, axis=-1, keepdims=True))
            alpha = jnp.exp2(m - m_new)
            p = jnp.exp2(s - m_new)
            l = alpha * l + jnp.sum(p, axis=-1, keepdims=True)
            acc = alpha * acc + _dot(p.astype(BF16), v)
            new.append((m_new, l, acc))
        return tuple(new)

    init = tuple((jnp.full((tq, 1), -jnp.inf, F32), jnp.zeros((tq, 1), F32),
                  jnp.zeros((tq, V7X_LANES), F32)) for _ in range(2))
    carry = lax.fori_loop(0, qi, functools.partial(step, masked=False), init)
    carry = step(qi, carry, True)
    outs = [acc / l for (_, l, acc) in carry]
    o_ref[0] = _pair_rmsnorm(outs, lane, hd, g_ref[...]).astype(BF16)


def _sb_kernel(q_ref, k_ref, v_ref, tri_ref, g_ref, o_ref, *, tq, sub, hd):
    qi = pl.program_id(2)
    q = q_ref[0]
    lane = lax.broadcasted_iota(jnp.int32, (1, V7X_LANES), 1)
    row = lax.broadcasted_iota(jnp.int32, (tq, tq), 0)
    col = lax.broadcasted_iota(jnp.int32, (tq, tq), 1)
    qh = [jnp.where((lane // hd) == h, q, jnp.zeros_like(q)) for h in range(2)]
    n_sub = tq // sub

    def step(kb, carry, masked):
        k0 = pl.multiple_of(kb * tq, tq)
        k = k_ref[0, pl.ds(k0, tq), :]
        v = v_ref[0, pl.ds(k0, tq), :]
        tri = tri_ref[...]
        new = []
        for h in range(2):
            rsum, acc = carry[h]
            z = _dot_nt(qh[h], k)
            sp = jnp.maximum(z, 0.0) + _softplus_neg_abs(z)
            lsz = z - sp
            if masked:
                valid = col < row
                sp = jnp.where(valid, sp, 0.0)
            afters = []
            later = rsum
            for c in reversed(range(n_sub)):
                part = sp[:, c * sub:(c + 1) * sub]
                afters.append(_dot(part.astype(BF16), tri) + later)
                later = later + jnp.sum(part, axis=-1, keepdims=True)
            after = afters[0] if n_sub == 1 else jnp.concatenate(afters[::-1], axis=-1)
            a = jnp.exp(lsz - after)
            if masked:
                a = jnp.where(valid, a, 0.0)
            new.append((later, acc + _dot(a.astype(BF16), v)))
        return tuple(new)

    init = tuple((jnp.zeros((tq, 1), F32), jnp.zeros((tq, V7X_LANES), F32)) for _ in range(2))
    carry = step(qi, init, True)
    carry = lax.fori_loop(0, qi, lambda i, c: step(qi - 1 - i, c, False), carry)
    outs = [acc for (_, acc) in carry]
    o_ref[0] = _pair_rmsnorm(outs, lane, hd, g_ref[...]).astype(BF16)


def _prompt_attention(q, kb, vb, cum4, g_mix2, *, nf, hd):
    b, s, mix = q.shape
    nh = mix // hd
    fox_pairs = nf // 2
    sb_pairs = (nh - nf) // 2
    tq_fox = min(s, 1024)
    tq_sb = min(s, 512)
    sub = min(tq_sb, V7X_MXU_DIM)
    assert s % tq_fox == 0 and s % tq_sb == 0 and tq_sb % sub == 0
    tri = (jnp.arange(sub)[:, None] > jnp.arange(sub)[None, :]).astype(BF16)
    params = pltpu.CompilerParams(
        dimension_semantics=("arbitrary", "arbitrary", "arbitrary"),
        vmem_limit_bytes=V7X_VMEM_BUDGET)

    def specs(off, tq):
        return (pl.BlockSpec((1, tq, V7X_LANES), lambda bi, p, i: (bi, i, p + off)),
                pl.BlockSpec((1, s, V7X_LANES), lambda bi, p, i: (bi, 0, p + off)),
                pl.BlockSpec((1, s, V7X_LANES), lambda bi, p, i: (bi, 0, p + off)),
                pl.BlockSpec((1, V7X_LANES), lambda bi, p, i: (0, p + off)),
                pl.BlockSpec((1, tq, V7X_LANES), lambda bi, p, i: (bi, i, p)))

    qs, ks, vs, gs, os_ = specs(0, tq_fox)
    o_fox = pl.pallas_call(
        functools.partial(_fox_kernel, tq=tq_fox, hd=hd),
        out_shape=jax.ShapeDtypeStruct((b, s, fox_pairs * V7X_LANES), BF16),
        grid=(b, fox_pairs, s // tq_fox),
        in_specs=[qs, ks, vs,
                  pl.BlockSpec((1, 1, 2, s), lambda bi, p, i: (bi, p, 0, 0)),
                  gs],
        out_specs=os_,
        compiler_params=params,
        name="fox_prompt",
    )(q, kb, vb, cum4, g_mix2)
    qs, ks, vs, gs, os_ = specs(fox_pairs, tq_sb)
    o_sb = pl.pallas_call(
        functools.partial(_sb_kernel, tq=tq_sb, sub=sub, hd=hd),
        out_shape=jax.ShapeDtypeStruct((b, s, sb_pairs * V7X_LANES), BF16),
        grid=(b, sb_pairs, s // tq_sb),
        in_specs=[qs, ks, vs, pl.BlockSpec((sub, sub), lambda bi, p, i: (0, 0)), gs],
        out_specs=os_,
        compiler_params=params,
        name="sb_prompt",
    )(q, kb, vb, tri, g_mix2)
    return jnp.concatenate([o_fox, o_sb], axis=-1)


def _decode_kernel(pt_ref, q_ref, knt_ref, vnt_ref, lfn_ref, tri_ref, g_ref, *rest,
                   pps, n_tok, nh, nf, hd):
    del pt_ref
    k_refs = rest[0:pps]
    v_refs = rest[pps:2 * pps]
    lf_refs = rest[2 * pps:3 * pps]
    o_ref = rest[3 * pps]
    qbd_ref, m_ref, l_ref, r_ref, gc_ref, acc_ref = rest[3 * pps + 1:]
    j = pl.program_id(1)
    rows = n_tok * nh
    ps = tri_ref.shape[1]
    mix = nh * hd
    rowi = lax.broadcasted_iota(jnp.int32, (rows, 1), 0)
    row_h = rowi % nh
    row_t = rowi // nh
    is_fox = row_h < nf
    key = lax.broadcasted_iota(jnp.int32, (1, ps), 1)
    lane = lax.broadcasted_iota(jnp.int32, (1, mix), 1)

    def process(blocks, new):
        tri = tri_ref[...]
        qbd = qbd_ref[...]
        g_off = gc_ref[0:nf, 0:1]
        r_off = r_ref[:, 0:1]
        xs, a_sbs = [], []
        for k_t, _, lf_t in blocks:
            s = _dot(qbd, k_t)
            lf = lf_t[0:nf]
            g_suffix = (_dot_parts(_split3(lf), tri) + g_off) * LOG2E
            g_off = g_off + jnp.sum(lf, axis=-1, keepdims=True)
            if nh > nf:
                g_suffix = jnp.concatenate([g_suffix, jnp.zeros((nh - nf, ps), F32)], axis=0)
            x = s + jnp.concatenate([g_suffix] * n_tok, axis=0)
            if new:
                x = jnp.where((key < n_tok) & (key <= row_t), x, -jnp.inf)
            xs.append(x)
            sp = jnp.maximum(s, 0.0) + _softplus_neg_abs(s)
            lsz = s - sp
            if new:
                valid = key < row_t
                sp = jnp.where(valid, sp, 0.0)
            a_sb = jnp.exp(lsz - (_dot(sp.astype(BF16), tri[0:ps]) + r_off))
            if new:
                a_sb = jnp.where(valid, a_sb, 0.0)
            a_sbs.append(a_sb)
            r_off = r_off + jnp.sum(sp, axis=-1, keepdims=True)
        gc_ref[0:nf, :] = jnp.broadcast_to(g_off, (nf, gc_ref.shape[1]))
        r_ref[...] = jnp.broadcast_to(r_off, r_ref.shape)
        m_prev = m_ref[:, 0:1]
        m_new = m_prev
        for x in xs:
            m_new = jnp.maximum(m_new, jnp.max(x, axis=-1, keepdims=True))
        alpha = jnp.exp2(m_prev - m_new)
        l_new = alpha * l_ref[:, 0:1]
        pv = None
        for x, a_sb, (_, v_t, _) in zip(xs, a_sbs, blocks):
            p_fox = jnp.exp2(x - m_new)
            l_new = l_new + jnp.sum(p_fox, axis=-1, keepdims=True)
            d = _dot_nt(jnp.where(is_fox, p_fox, a_sb).astype(BF16), v_t)
            pv = d if pv is None else pv + d
        m_ref[...] = jnp.broadcast_to(m_new, m_ref.shape)
        l_ref[...] = jnp.broadcast_to(l_new, l_ref.shape)
        acc_ref[...] = acc_ref[...] * jnp.where(is_fox, alpha, 1.0) + pv

    @pl.when(j == 0)
    def _():
        q = q_ref[0]
        head_rows = lax.broadcasted_iota(jnp.int32, (nh, 1), 0)
        own = (lane // hd) == head_rows
        qbd_ref[...] = jnp.concatenate(
            [jnp.where(own, jnp.broadcast_to(q[t:t + 1], (nh, mix)), 0.0) for t in range(n_tok)],
            axis=0).astype(BF16)
        m_ref[...] = jnp.full(m_ref.shape, -jnp.inf, F32)
        l_ref[...] = jnp.zeros_like(l_ref)
        r_ref[...] = jnp.zeros_like(r_ref)
        gc_ref[...] = jnp.zeros_like(gc_ref)
        acc_ref[...] = jnp.zeros_like(acc_ref)
        process([(knt_ref[0], vnt_ref[0], lfn_ref[0])], True)

    process([(k_refs[i][0, 0].astype(BF16), v_refs[i][0, 0].astype(BF16), lf_refs[i][0, 0])
             for i in range(pps)], False)

    @pl.when(j == pl.num_programs(1) - 1)
    def _():
        o = acc_ref[...] * jnp.where(is_fox, 1.0 / l_ref[:, 0:1], 1.0)
        own = (lane // hd) == row_h
        ms = jnp.sum(jnp.where(own, o * o, 0.0), axis=-1, keepdims=True) * (1.0 / hd)
        on = jnp.where(own, o * lax.rsqrt(ms + RMS_EPS), 0.0)
        outs = [jnp.sum(on[t * nh:(t + 1) * nh], axis=0, keepdims=True) for t in range(n_tok)]
        o_ref[0] = (jnp.concatenate(outs, axis=0) * g_ref[...]).astype(BF16)


def _decode_attention(q, k_new_t, v_new_t, lf_new_t, cache_kt, cache_vt, cache_lf_t, page_table,
                      g_mix2, *, layer, nh, nf, hd):
    db, n_tok, mix = q.shape
    n_pages = page_table.shape[1]
    nfp, ps = cache_lf_t.shape[2:]
    pps = max(p for p in (16, 8, 4, 2, 1) if n_pages % p == 0)
    rows = n_tok * nh
    tri = _stack_rows((jnp.arange(ps)[:, None] > jnp.arange(ps)[None, :]).astype(BF16), 3)
    per_b = lambda b, j, pt: (b, 0, 0)
    const = lambda b, j, pt: (0, 0)

    def page_map(i):
        return lambda b, j, pt: (pt[b, n_pages - 1 - (j * pps + i)], layer, 0, 0)

    in_specs = [pl.BlockSpec((1, n_tok, mix), per_b),
                pl.BlockSpec((1, mix, ps), per_b),
                pl.BlockSpec((1, mix, ps), per_b),
                pl.BlockSpec((1, nfp, ps), per_b),
                pl.BlockSpec((3 * ps, ps), const),
                pl.BlockSpec((1, mix), const)]
    in_specs += [pl.BlockSpec((1, 1, mix, ps), page_map(i)) for i in range(pps)]
    in_specs += [pl.BlockSpec((1, 1, mix, ps), page_map(i)) for i in range(pps)]
    in_specs += [pl.BlockSpec((1, 1, nfp, ps), page_map(i)) for i in range(pps)]
    grid_spec = pltpu.PrefetchScalarGridSpec(
        num_scalar_prefetch=1,
        grid=(db, n_pages // pps),
        in_specs=in_specs,
        out_specs=pl.BlockSpec((1, n_tok, mix), per_b),
        scratch_shapes=[pltpu.VMEM((rows, mix), BF16),
                        pltpu.VMEM((rows, V7X_LANES), F32),
                        pltpu.VMEM((rows, V7X_LANES), F32),
                        pltpu.VMEM((rows, V7X_LANES), F32),
                        pltpu.VMEM((nfp, V7X_LANES), F32),
                        pltpu.VMEM((rows, mix), F32)])
    return pl.pallas_call(
        functools.partial(_decode_kernel, pps=pps, n_tok=n_tok, nh=nh, nf=nf, hd=hd),
        out_shape=jax.ShapeDtypeStruct((db, n_tok, mix), BF16),
        grid_spec=grid_spec,
        compiler_params=pltpu.CompilerParams(
            dimension_semantics=("arbitrary", "arbitrary"),
            vmem_limit_bytes=V7X_VMEM_BUDGET),
        name="decode_attn",
    )(page_table, q, k_new_t, v_new_t, lf_new_t, tri, g_mix2,
      *([cache_kt] * pps), *([cache_vt] * pps), *([cache_lf_t] * pps))


def _outproj_kernel(mix_ref, x_ref, g1_ref, sc_ref, sh_ref, w_ref, lg_ref, lb_ref, wr_ref, br_ref,
                    x1_ref, h2_ref, gate_ref, *, alpha, ng, epg):
    mo = _dot(mix_ref[0], w_ref[...])
    x1 = _layer_norm(alpha * x_ref[0] + g1_ref[0] * mo, lg_ref[...], lb_ref[...])
    x1_ref[0] = x1
    h2 = x1 * (1.0 + sc_ref[0]) + sh_ref[0]
    h2_ref[0] = h2.astype(BF16)
    h_hi, h_lo = _split2(h2)
    logits = _dot(jnp.concatenate([h_hi, h_lo, h_hi], axis=-1), wr_ref[...]) + br_ref[...]
    lane = lax.broadcasted_iota(jnp.int32, logits.shape, 1)
    big = jnp.int32(2 ** 30)
    neg = -jnp.inf
    gl = jnp.where(lane < ng, logits, neg)
    gmax = jnp.max(gl, axis=-1, keepdims=True)
    gidx = jnp.min(jnp.where(gl == gmax, lane, big), axis=-1, keepdims=True)
    g_w = 1.0 / jnp.sum(jnp.exp(gl - gmax), axis=-1, keepdims=True)
    in_group = (lane >= ng + gidx * epg) & (lane < ng + (gidx + 1) * epg)
    el = jnp.where(in_group, logits, neg)
    m1 = jnp.max(el, axis=-1, keepdims=True)
    i1 = jnp.min(jnp.where(el == m1, lane, big), axis=-1, keepdims=True)
    el2 = jnp.where(lane == i1, neg, el)
    m2 = jnp.max(el2, axis=-1, keepdims=True)
    i2 = jnp.min(jnp.where(el2 == m2, lane, big), axis=-1, keepdims=True)
    den = jnp.sum(jnp.exp(el - m1), axis=-1, keepdims=True)
    p1 = 1.0 / den
    p2 = jnp.exp(m2 - m1) / den
    w1 = p1 / (p1 + p2)
    w2 = p2 / (p1 + p2)
    gate_ref[0] = jnp.where(lane == i1, g_w * w1, jnp.where(lane == i2, g_w * w2, 0.0))


def _outproj(mixn, x, g1, sc2, sh2, w_out, ln_g, ln_b, w_r, b_r, *, alpha, ng, epg):
    nb, s, d = x.shape
    mix = mixn.shape[2]
    mrows = g1.shape[1]
    ts = min(s, 512)
    assert s % ts == 0
    tok = lambda b, i: (b, i, 0)
    mod = lambda b, i: (b, 0, 0)
    const = lambda b, i: (0, 0)
    return pl.pallas_call(
        functools.partial(_outproj_kernel, alpha=alpha, ng=ng, epg=epg),
        out_shape=(jax.ShapeDtypeStruct((nb, s, d), F32),
                   jax.ShapeDtypeStruct((nb, s, d), BF16),
                   jax.ShapeDtypeStruct((nb, s, V7X_LANES), F32)),
        grid=(nb, s // ts),
        in_specs=[pl.BlockSpec((1, ts, mix), tok),
                  pl.BlockSpec((1, ts, d), tok),
                  pl.BlockSpec((1, mrows, d), mod),
                  pl.BlockSpec((1, mrows, d), mod),
                  pl.BlockSpec((1, mrows, d), mod),
                  pl.BlockSpec((mix, d), const),
                  pl.BlockSpec((1, d), const),
                  pl.BlockSpec((1, d), const),
                  pl.BlockSpec((3 * d, V7X_LANES), const),
                  pl.BlockSpec((1, V7X_LANES), const)],
        out_specs=(pl.BlockSpec((1, ts, d), tok),
                   pl.BlockSpec((1, ts, d), tok),
                   pl.BlockSpec((1, ts, V7X_LANES), tok)),
        compiler_params=pltpu.CompilerParams(
            dimension_semantics=("arbitrary", "arbitrary"),
            vmem_limit_bytes=V7X_VMEM_BUDGET),
        name="outproj_router",
    )(mixn, x, g1, sc2, sh2, w_out, ln_g, ln_b, w_r, b_r)


def _moe_kernel(h_ref, gate_ref, x1_ref, g2_ref, wg_ref, wu_ref, wd_ref, lg_ref, lb_ref,
                y_ref, acc_ref, *, alpha, ng, ep):
    e = pl.program_id(2)

    @pl.when(e == 0)
    def _():
        acc_ref[...] = jnp.zeros_like(acc_ref)

    h = h_ref[0]
    gates = gate_ref[0]
    lane = lax.broadcasted_iota(jnp.int32, gates.shape, 1)
    zs = []
    for i in range(ep):
        a = _dot(h, wg_ref[i])
        u = _dot(h, wu_ref[i])
        gcol = jnp.sum(jnp.where(lane == ng + e * ep + i, gates, 0.0), axis=-1, keepdims=True)
        zs.append((a * (1.0 / (1.0 + jnp.exp(-a))) * u * gcol).astype(BF16))
    zz = zs[0] if ep == 1 else jnp.concatenate(zs, axis=-1)
    acc_ref[...] += _dot(zz, wd_ref[...])

    @pl.when(e == pl.num_programs(2) - 1)
    def _():
        y_ref[0] = _layer_norm(alpha * x1_ref[0] + g2_ref[0] * acc_ref[...],
                               lg_ref[...], lb_ref[...])


def _moe(h2, gates, x1, g2, w_gate, w_up, w_down2, ln_g, ln_b, *, alpha, ng):
    nb, s, d = x1.shape
    ne, _, f = w_gate.shape
    mrows = g2.shape[1]
    tm = min(s, 1024)
    assert s % tm == 0
    ep = 4 if ne % 4 == 0 else 1
    tok = lambda b, i, e: (b, i, 0)
    mod = lambda b, i, e: (b, 0, 0)
    const = lambda b, i, e: (0, 0)
    return pl.pallas_call(
        functools.partial(_moe_kernel, alpha=alpha, ng=ng, ep=ep),
        out_shape=jax.ShapeDtypeStruct((nb, s, d), F32),
        grid=(nb, s // tm, ne // ep),
        in_specs=[pl.BlockSpec((1, tm, d), tok),
                  pl.BlockSpec((1, tm, V7X_LANES), tok),
                  pl.BlockSpec((1, tm, d), tok),
                  pl.BlockSpec((1, mrows, d), mod),
                  pl.BlockSpec((ep, d, f), lambda b, i, e: (e, 0, 0)),
                  pl.BlockSpec((ep, d, f), lambda b, i, e: (e, 0, 0)),
                  pl.BlockSpec((ep * f, d), lambda b, i, e: (e, 0)),
                  pl.BlockSpec((1, d), const),
                  pl.BlockSpec((1, d), const)],
        out_specs=pl.BlockSpec((1, tm, d), tok),
        scratch_shapes=[pltpu.VMEM((tm, d), F32)],
        compiler_params=pltpu.CompilerParams(
            dimension_semantics=("arbitrary", "arbitrary", "arbitrary"),
            vmem_limit_bytes=V7X_VMEM_BUDGET),
        name="moe_experts",
    )(h2, gates, x1, g2, w_gate, w_up, w_down2, ln_g, ln_b)


def _pad_to(a, axis, size):
    if a.shape[axis] == size:
        return a
    pad = [(0, 0)] * a.ndim
    pad[axis] = (0, size - a.shape[axis])
    return jnp.pad(a, pad)


def kernel(x_prompt, x_sample, c_prompt, c_sample, cache_k, cache_v, cache_logf, page_table, w_in, b_forget, g_mix, w_out, w_ada, b_ada, ln1_g, ln1_b, ln2_g, ln2_b, w_router_group, b_router_group, w_router_expert, b_router_expert, w_gate_e, w_up_e, w_down_e):
    b, s, d = x_prompt.shape
    db, n_tok, _ = x_sample.shape
    depth = w_in.shape[0]
    n_pool, _, ps, nh, hd = cache_k.shape
    nf = b_forget.shape[1]
    mix = nh * hd
    ng = w_router_group.shape[2]
    ne = w_router_expert.shape[2]
    epg = ne // ng
    f = w_gate_e.shape[3]
    nfp = _round_up(nf, V7X_SUBLANES)
    alpha = float((2 * depth) ** 0.25)
    assert nf % 2 == 0 and (nh - nf) % 2 == 0 and 2 * hd == V7X_LANES
    assert ng + ne <= V7X_LANES and n_tok <= ps

    cache_kt = jnp.transpose(cache_k, (0, 1, 3, 4, 2)).reshape(n_pool, depth, mix, ps)
    cache_vt = jnp.transpose(cache_v, (0, 1, 3, 4, 2)).reshape(n_pool, depth, mix, ps)
    cache_lf_t = _pad_to(jnp.swapaxes(cache_logf, 2, 3), 2, nfp)

    n_c = b + db
    c_all = _pad_to(jnp.concatenate([c_prompt, c_sample], axis=0), 0, _round_up(n_c, V7X_SUBLANES))

    yp, ys = x_prompt, x_sample.reshape(1, db * n_tok, d)
    outs = {name: [] for name in ("kp", "vp", "lp", "ks", "vs", "ls")}
    for l in range(depth):
        mods = _adaln(c_all, w_ada[l], b_ada[l])
        mods_p = [m.reshape(b, 1, d) for m in jnp.split(mods[:b], 6, axis=-1)]
        mods_s = [jnp.repeat(m, n_tok, axis=0).reshape(1, db * n_tok, d)
                  for m in jnp.split(mods[b:n_c], 6, axis=-1)]

        w_qkv = w_in[l][:, :3 * mix].astype(BF16)
        w_f = w_in[l][:, 3 * mix:]
        wf = _pad_to(w_f, 1, V7X_LANES).astype(BF16)
        wft = _pad_to(w_f.T, 0, nfp).astype(BF16)
        bf = b_forget[l].reshape(1, nf)
        bft = _pad_to(b_forget[l].reshape(nf, 1), 0, nfp)
        g_mix2 = g_mix[l].reshape(1, mix)
        w_out_b = w_out[l].astype(BF16)
        w_r = _pad_to(jnp.concatenate([w_router_group[l], w_router_expert[l]], axis=1), 1, V7X_LANES)
        w_r_hi, w_r_lo = _split2(w_r)
        w_r = jnp.concatenate([w_r_hi, w_r_hi, w_r_lo], axis=0)
        b_r = _pad_to(jnp.concatenate([b_router_group[l], b_router_expert[l]]).reshape(1, ng + ne),
                      1, V7X_LANES)
        wg_b = w_gate_e[l].astype(BF16)
        wu_b = w_up_e[l].astype(BF16)
        wd_b = w_down_e[l].astype(BF16).reshape(ne * f, d)
        lg1, lb1 = ln1_g[l].reshape(1, d), ln1_b[l].reshape(1, d)
        lg2, lb2 = ln2_g[l].reshape(1, d), ln2_b[l].reshape(1, d)
        proj = functools.partial(_inproj, w_qkv=w_qkv, wf=wf, wft=wft, bf=bf, bft=bft,
                                 mix=mix, nf=nf, hd=hd)
        ffn = functools.partial(_moe, w_gate=wg_b, w_up=wu_b, w_down2=wd_b, ln_g=lg2, ln_b=lb2,
                                alpha=alpha, ng=ng)
        mixer_out = functools.partial(_outproj, w_out=w_out_b, ln_g=lg1, ln_b=lb1, w_r=w_r, b_r=b_r,
                                      alpha=alpha, ng=ng, epg=epg)

        sh1, sc1, g1, sh2, sc2, g2 = mods_p
        q, k, v, kb, vb, lf, cum = proj(yp, sc1, sh1)
        cum4 = cum[:, :nf].reshape(b, nf // 2, 2, s)
        mixn = _prompt_attention(q, kb, vb, cum4, g_mix2, nf=nf, hd=hd)
        x1, h2, gates = mixer_out(mixn, yp, g1, sc2, sh2)
        yp = ffn(h2, gates, x1, g2)
        outs["kp"].append(k.reshape(b, s, nh, hd))
        outs["vp"].append(v.reshape(b, s, nh, hd))
        outs["lp"].append(lf)

        sh1, sc1, g1, sh2, sc2, g2 = mods_s
        q, k, v, kb, vb, lf, _ = proj(ys, sc1, sh1)
        q_s = q.reshape(db, n_tok, mix).astype(F32)
        k_new_t = _pad_to(jnp.swapaxes(kb.reshape(db, n_tok, mix), 1, 2), 2, ps)
        v_new_t = _pad_to(jnp.swapaxes(vb.reshape(db, n_tok, mix), 1, 2), 2, ps)
        lf_new_t = _pad_to(_pad_to(jnp.swapaxes(lf.reshape(db, n_tok, nf), 1, 2), 1, nfp), 2, ps)
        mixn = _decode_attention(q_s, k_new_t, v_new_t, lf_new_t, cache_kt, cache_vt, cache_lf_t,
                                 page_table, g_mix2, layer=l, nh=nh, nf=nf, hd=hd)
        x1, h2, gates = mixer_out(mixn.reshape(1, db * n_tok, mix), ys, g1, sc2, sh2)
        ys = ffn(h2, gates, x1, g2)
        outs["ks"].append(k.reshape(db, n_tok, nh, hd))
        outs["vs"].append(v.reshape(db, n_tok, nh, hd))
        outs["ls"].append(lf.reshape(db, n_tok, nf))

    stack = lambda name: jnp.stack(outs[name], axis=1)
    return (yp, ys.reshape(db, n_tok, d), stack("kp"), stack("vp"), stack("lp"),
            stack("ks"), stack("vs"), stack("ls"))
```

```python
import functools

import jax
import jax.numpy as jnp
from jax import lax
from jax.experimental import pallas as pl
from jax.experimental.pallas import tpu as pltpu

F32 = jnp.float32
BF16 = jnp.bfloat16
LN_EPS = 1e-5
RMS_EPS = 1e-6
LOG2E = 1.4426950408889634
SB_DEAD_MASS = 104.0

V7X_LANES = 128
V7X_SUBLANES = 8
V7X_MXU_DIM = 256
V7X_VMEM_BUDGET = 56 * 1024 * 1024

_NT = (((1,), (1,)), ((), ()))


def _dot(a, b):
    return jnp.dot(a, b, preferred_element_type=F32)


def _dot_nt(a, b):
    return lax.dot_general(a, b, _NT, preferred_element_type=F32)


def _round_up(n, m):
    return (n + m - 1) // m * m


def _split3(x):
    hi = x.astype(BF16)
    r1 = x - hi.astype(F32)
    mid = r1.astype(BF16)
    lo = (r1 - mid.astype(F32)).astype(BF16)
    return hi, mid, lo


def _split2(x):
    hi = x.astype(BF16)
    lo = (x - hi.astype(F32)).astype(BF16)
    return hi, lo


def _dot_parts(parts, tri_stack):
    width = tri_stack.shape[1]
    return _dot(jnp.concatenate(parts, axis=-1), tri_stack[0:len(parts) * width])


def _stack_rows(tri, n):
    return jnp.concatenate([tri] * n, axis=0)


def _softplus_neg_abs(z):
    return jnp.log(1.0 + jnp.exp2(jnp.abs(z) * (-LOG2E)))


def _layer_norm(x, g, b):
    mu = jnp.mean(x, axis=-1, keepdims=True)
    xc = x - mu
    var = jnp.mean(xc * xc, axis=-1, keepdims=True)
    return xc * lax.rsqrt(var + LN_EPS) * g + b


def _adaln_kernel(c_ref, w_ref, b_ref, o_ref):
    c = c_ref[...]
    sc = c * (1.0 / (1.0 + jnp.exp(-c)))
    o_ref[...] = jnp.dot(sc, w_ref[...], precision=lax.Precision.HIGHEST,
                         preferred_element_type=F32) + b_ref[...]


def _adaln(c_all, w_ada, b_ada):
    rows, d = c_all.shape
    n = w_ada.shape[1]
    tn = min(n, 512)
    return pl.pallas_call(
        _adaln_kernel,
        out_shape=jax.ShapeDtypeStruct((rows, n), F32),
        grid=(n // tn,),
        in_specs=[pl.BlockSpec((rows, d), lambda j: (0, 0)),
                  pl.BlockSpec((d, tn), lambda j: (0, j)),
                  pl.BlockSpec((1, tn), lambda j: (0, j))],
        out_specs=pl.BlockSpec((rows, tn), lambda j: (0, j)),
        compiler_params=pltpu.CompilerParams(dimension_semantics=("arbitrary",)),
        name="adaln",
    )(c_all, w_ada, b_ada.reshape(1, n))


def _inproj_kernel(x_ref, sc_ref, sh_ref, w_ref, wf_ref, wft_ref, bf_ref, bft_ref, tri_ref, qs_ref,
                   q_ref, k_ref, v_ref, kb_ref, vb_ref, lf_ref, cum_ref, carry_ref,
                   *, mix, nf):
    si = pl.program_id(1)
    h = (x_ref[0] * (1.0 + sc_ref[0]) + sh_ref[0]).astype(BF16)
    q_ref[0] = (_dot(h, w_ref[:, 0:mix]) * qs_ref[...]).astype(BF16)
    k = _dot(h, w_ref[:, mix:2 * mix])
    k_ref[0] = k
    kb_ref[0] = k.astype(BF16)
    v = _dot(h, w_ref[:, 2 * mix:3 * mix])
    v_ref[0] = v
    vb_ref[0] = v.astype(BF16)

    logit = _dot(h, wf_ref[...])[:, 0:nf] + bf_ref[...]
    lf_ref[0] = jnp.minimum(logit, 0.0) - _softplus_neg_abs(logit)

    logit_t = _dot_nt(wft_ref[...], h) + bft_ref[...]
    lf_t = jnp.minimum(logit_t, 0.0) - _softplus_neg_abs(logit_t)
    local = _dot_parts(_split3(lf_t), tri_ref[...])

    @pl.when(si == 0)
    def _():
        carry_ref[...] = jnp.zeros_like(carry_ref)

    cum = local + carry_ref[:, 0:1]
    cum_ref[0] = cum
    ts = cum.shape[1]
    carry_ref[...] = jnp.broadcast_to(cum[:, ts - 1:ts], carry_ref.shape)


def _inproj(x, sc, sh, w_qkv, wf, wft, bf, bft, *, mix, nf, hd):
    nb, s, d = x.shape
    mrows = sc.shape[1]
    ts = min(s, 512)
    assert s % ts == 0
    nfp = wft.shape[0]
    tri = _stack_rows((jnp.arange(ts)[:, None] <= jnp.arange(ts)[None, :]).astype(BF16), 3)
    col_head = jnp.arange(mix) // hd
    q_scale = (jnp.where(col_head < nf, LOG2E, 1.0) * float(hd) ** -0.5).astype(F32).reshape(1, mix)
    tok = lambda b, i: (b, i, 0)
    mod = lambda b, i: (b, 0, 0)
    const = lambda b, i: (0, 0)
    return pl.pallas_call(
        functools.partial(_inproj_kernel, mix=mix, nf=nf),
        out_shape=(jax.ShapeDtypeStruct((nb, s, mix), BF16),
                   jax.ShapeDtypeStruct((nb, s, mix), F32),
                   jax.ShapeDtypeStruct((nb, s, mix), F32),
                   jax.ShapeDtypeStruct((nb, s, mix), BF16),
                   jax.ShapeDtypeStruct((nb, s, mix), BF16),
                   jax.ShapeDtypeStruct((nb, s, nf), F32),
                   jax.ShapeDtypeStruct((nb, nfp, s), F32)),
        grid=(nb, s // ts),
        in_specs=[pl.BlockSpec((1, ts, d), tok),
                  pl.BlockSpec((1, mrows, d), mod),
                  pl.BlockSpec((1, mrows, d), mod),
                  pl.BlockSpec((d, 3 * mix), const),
                  pl.BlockSpec((d, V7X_LANES), const),
                  pl.BlockSpec((nfp, d), const),
                  pl.BlockSpec((1, nf), const),
                  pl.BlockSpec((nfp, 1), const),
                  pl.BlockSpec((3 * ts, ts), const),
                  pl.BlockSpec((1, mix), const)],
        out_specs=(pl.BlockSpec((1, ts, mix), tok),
                   pl.BlockSpec((1, ts, mix), tok),
                   pl.BlockSpec((1, ts, mix), tok),
                   pl.BlockSpec((1, ts, mix), tok),
                   pl.BlockSpec((1, ts, mix), tok),
                   pl.BlockSpec((1, ts, nf), tok),
                   pl.BlockSpec((1, nfp, ts), lambda b, i: (b, 0, i))),
        scratch_shapes=[pltpu.VMEM((nfp, V7X_LANES), F32)],
        compiler_params=pltpu.CompilerParams(
            dimension_semantics=("arbitrary", "arbitrary"),
            vmem_limit_bytes=V7X_VMEM_BUDGET),
        name="inproj",
    )(x, sc, sh, w_qkv, wf, wft, bf, bft, tri, q_scale)


def _pair_rmsnorm(outs, lane, hd, g):
    first = (lane // hd) == 0
    o = jnp.where(first, outs[0], outs[1])
    osq = o * o
    ms0 = jnp.sum(jnp.where(first, osq, 0.0), axis=-1, keepdims=True)
    ms1 = jnp.sum(jnp.where(first, 0.0, osq), axis=-1, keepdims=True)
    ms = jnp.where(first, ms0, ms1) * (1.0 / hd)
    return o * lax.rsqrt(ms + RMS_EPS) * g


def _fox_kernel(q_ref, k_ref, v_ref, cum_ref, g_ref, o_ref, *, tq, hd):
    qi = pl.program_id(2)
    q = q_ref[0]
    lane = lax.broadcasted_iota(jnp.int32, (1, V7X_LANES), 1)
    row = lax.broadcasted_iota(jnp.int32, (tq, tq), 0)
    col = lax.broadcasted_iota(jnp.int32, (tq, tq), 1)
    qh = [jnp.where((lane // hd) == h, q, jnp.zeros_like(q)) for h in range(2)]

    def step(kb, carry, masked):
        k0 = pl.multiple_of(kb * tq, tq)
        k = k_ref[0, pl.ds(k0, tq), :]
        v = v_ref[0, pl.ds(k0, tq), :]
        new = []
        for h in range(2):
            m, l, acc = carry[h]
            s = _dot_nt(qh[h], k) - cum_ref[0, 0, h:h + 1, pl.ds(k0, tq)] * LOG2E
            if masked:
                s = jnp.where(col <= row, s, -jnp.inf)
            m_new = jnp.maximum(m, jnp.max(s, axis=-1, keepdims=True))
            alpha = jnp.exp2(m - m_new)
            p = jnp.exp2(s - m_new)
            l = alpha * l + jnp.sum(p, axis=-1, keepdims=True)
            acc = alpha * acc + _dot(p.astype(BF16), v)
            new.append((m_new, l, acc))
        return tuple(new)

    init = tuple((jnp.full((tq, 1), -jnp.inf, F32), jnp.zeros((tq, 1), F32),
                  jnp.zeros((tq, V7X_LANES), F32)) for _ in range(2))
    carry = lax.fori_loop(0, qi, functools.partial(step, masked=False), init)
    carry = step(qi, carry, True)
    outs = [acc / l for (_, l, acc) in carry]
    o_ref[0] = _pair_rmsnorm(outs, lane, hd, g_ref[...]).astype(BF16)


def _sb_kernel(q_ref, k_ref, v_ref, tri_ref, g_ref, o_ref, *, tq, sub, hd):
    qi = pl.program_id(2)
    q = q_ref[0]
    lane = lax.broadcasted_iota(jnp.int32, (1, V7X_LANES), 1)
    row = lax.broadcasted_iota(jnp.int32, (tq, tq), 0)
    col = lax.broadcasted_iota(jnp.int32, (tq, tq), 1)
    qh = [jnp.where((lane // hd) == h, q, jnp.zeros_like(q)) for h in range(2)]
    n_sub = tq // sub

    def step(kb, carry, masked):
        k0 = pl.multiple_of(kb * tq, tq)
        k = k_ref[0, pl.ds(k0, tq), :]
        v = v_ref[0, pl.ds(k0, tq), :]
        tri = tri_ref[...]
        new = []
        for h in range(2):
            rsum, acc = carry[h]
            z = _dot_nt(qh[h], k)
            sp = jnp.maximum(z, 0.0) + _softplus_neg_abs(z)
            lsz = z - sp
            if masked:
                valid = col < row
                sp = jnp.where(valid, sp, 0.0)
            afters = []
            later = rsum
            for c in reversed(range(n_sub)):
                part = sp[:, c * sub:(c + 1) * sub]
                afters.append(_dot(part.astype(BF16), tri) + later)
                later = later + jnp.sum(part, axis=-1, keepdims=True)
            after = afters[0] if n_sub == 1 else jnp.concatenate(afters[::-1], axis=-1)
            a = jnp.exp(lsz - after)
            if masked:
                a = jnp.where(valid, a, 0.0)
            new.append((later, acc + _dot(a.astype(BF16), v)))
        return tuple(new)

    def least_mass(carry):
        return jnp.minimum(jnp.min(carry[0][0]), jnp.min(carry[1][0]))

    def more_to_do(state):
        i, _, least = state
        return jnp.logical_and(i < qi, least < SB_DEAD_MASS)

    def earlier_block(state):
        i, carry, _ = state
        carry = step(qi - 1 - i, carry, False)
        return i + 1, carry, least_mass(carry)

    init = tuple((jnp.zeros((tq, 1), F32), jnp.zeros((tq, V7X_LANES), F32)) for _ in range(2))
    carry = step(qi, init, True)
    _, carry, _ = lax.while_loop(more_to_do, earlier_block, (jnp.int32(0), carry, least_mass(carry)))
    outs = [acc for (_, acc) in carry]
    o_ref[0] = _pair_rmsnorm(outs, lane, hd, g_ref[...]).astype(BF16)


def _prompt_attention(q, kb, vb, cum4, g_mix2, *, nf, hd):
    b, s, mix = q.shape
    nh = mix // hd
    fox_pairs = nf // 2
    sb_pairs = (nh - nf) // 2
    tq_fox = min(s, 1024)
    tq_sb = min(s, 512)
    sub = min(tq_sb, V7X_MXU_DIM)
    assert s % tq_fox == 0 and s % tq_sb == 0 and tq_sb % sub == 0
    tri = (jnp.arange(sub)[:, None] > jnp.arange(sub)[None, :]).astype(BF16)
    params = pltpu.CompilerParams(
        dimension_semantics=("arbitrary", "arbitrary", "arbitrary"),
        vmem_limit_bytes=V7X_VMEM_BUDGET)

    def specs(off, tq):
        return (pl.BlockSpec((1, tq, V7X_LANES), lambda bi, p, i: (bi, i, p + off)),
                pl.BlockSpec((1, s, V7X_LANES), lambda bi, p, i: (bi, 0, p + off)),
                pl.BlockSpec((1, s, V7X_LANES), lambda bi, p, i: (bi, 0, p + off)),
                pl.BlockSpec((1, V7X_LANES), lambda bi, p, i: (0, p + off)),
                pl.BlockSpec((1, tq, V7X_LANES), lambda bi, p, i: (bi, i, p)))

    qs, ks, vs, gs, os_ = specs(0, tq_fox)
    o_fox = pl.pallas_call(
        functools.partial(_fox_kernel, tq=tq_fox, hd=hd),
        out_shape=jax.ShapeDtypeStruct((b, s, fox_pairs * V7X_LANES), BF16),
        grid=(b, fox_pairs, s // tq_fox),
        in_specs=[qs, ks, vs,
                  pl.BlockSpec((1, 1, 2, s), lambda bi, p, i: (bi, p, 0, 0)),
                  gs],
        out_specs=os_,
        compiler_params=params,
        name="fox_prompt",
    )(q, kb, vb, cum4, g_mix2)
    qs, ks, vs, gs, os_ = specs(fox_pairs, tq_sb)
    o_sb = pl.pallas_call(
        functools.partial(_sb_kernel, tq=tq_sb, sub=sub, hd=hd),
        out_shape=jax.ShapeDtypeStruct((b, s, sb_pairs * V7X_LANES), BF16),
        grid=(b, sb_pairs, s // tq_sb),
        in_specs=[qs, ks, vs, pl.BlockSpec((sub, sub), lambda bi, p, i: (0, 0)), gs],
        out_specs=os_,
        compiler_params=params,
        name="sb_prompt",
    )(q, kb, vb, tri, g_mix2)
    return jnp.concatenate([o_fox, o_sb], axis=-1)


def _decode_kernel(pt_ref, q_ref, knt_ref, vnt_ref, lfn_ref, tri_ref, g_ref, *rest,
                   pps, n_tok, nh, nf, hd):
    del pt_ref
    k_refs = rest[0:pps]
    v_refs = rest[pps:2 * pps]
    lf_refs = rest[2 * pps:3 * pps]
    o_ref = rest[3 * pps]
    qbd_ref, m_ref, l_ref, r_ref, gc_ref, acc_ref = rest[3 * pps + 1:]
    j = pl.program_id(1)
    rows = n_tok * nh
    ps = tri_ref.shape[1]
    mix = nh * hd
    rowi = lax.broadcasted_iota(jnp.int32, (rows, 1), 0)
    row_h = rowi % nh
    row_t = rowi // nh
    is_fox = row_h < nf
    key = lax.broadcasted_iota(jnp.int32, (1, ps), 1)
    lane = lax.broadcasted_iota(jnp.int32, (1, mix), 1)

    def process(blocks, new):
        tri = tri_ref[...]
        qbd = qbd_ref[...]
        g_off = gc_ref[0:nf, 0:1]
        r_off = r_ref[:, 0:1]
        xs, a_sbs = [], []
        for k_t, _, lf_t in blocks:
            s = _dot(qbd, k_t)
            lf = lf_t[0:nf]
            g_suffix = (_dot_parts(_split3(lf), tri) + g_off) * LOG2E
            g_off = g_off + jnp.sum(lf, axis=-1, keepdims=True)
            if nh > nf:
                g_suffix = jnp.concatenate([g_suffix, jnp.zeros((nh - nf, ps), F32)], axis=0)
            x = s + jnp.concatenate([g_suffix] * n_tok, axis=0)
            if new:
                x = jnp.where((key < n_tok) & (key <= row_t), x, -jnp.inf)
            xs.append(x)
            sp = jnp.maximum(s, 0.0) + _softplus_neg_abs(s)
            lsz = s - sp
            if new:
                valid = key < row_t
                sp = jnp.where(valid, sp, 0.0)
            a_sb = jnp.exp(lsz - (_dot(sp.astype(BF16), tri[0:ps]) + r_off))
            if new:
                a_sb = jnp.where(valid, a_sb, 0.0)
            a_sbs.append(a_sb)
            r_off = r_off + jnp.sum(sp, axis=-1, keepdims=True)
        gc_ref[0:nf, :] = jnp.broadcast_to(g_off, (nf, gc_ref.shape[1]))
        r_ref[...] = jnp.broadcast_to(r_off, r_ref.shape)
        m_prev = m_ref[:, 0:1]
        m_new = m_prev
        for x in xs:
            m_new = jnp.maximum(m_new, jnp.max(x, axis=-1, keepdims=True))
        alpha = jnp.exp2(m_prev - m_new)
        l_new = alpha * l_ref[:, 0:1]
        pv = None
        for x, a_sb, (_, v_t, _) in zip(xs, a_sbs, blocks):
            p_fox = jnp.exp2(x - m_new)
            l_new = l_new + jnp.sum(p_fox, axis=-1, keepdims=True)
            d = _dot_nt(jnp.where(is_fox, p_fox, a_sb).astype(BF16), v_t)
            pv = d if pv is None else pv + d
        m_ref[...] = jnp.broadcast_to(m_new, m_ref.shape)
        l_ref[...] = jnp.broadcast_to(l_new, l_ref.shape)
        acc_ref[...] = acc_ref[...] * jnp.where(is_fox, alpha, 1.0) + pv

    @pl.when(j == 0)
    def _():
        q = q_ref[0]
        head_rows = lax.broadcasted_iota(jnp.int32, (nh, 1), 0)
        own = (lane // hd) == head_rows
        qbd_ref[...] = jnp.concatenate(
            [jnp.where(own, jnp.broadcast_to(q[t:t + 1], (nh, mix)), 0.0) for t in range(n_tok)],
            axis=0).astype(BF16)
        m_ref[...] = jnp.full(m_ref.shape, -jnp.inf, F32)
        l_ref[...] = jnp.zeros_like(l_ref)
        r_ref[...] = jnp.zeros_like(r_ref)
        gc_ref[...] = jnp.zeros_like(gc_ref)
        acc_ref[...] = jnp.zeros_like(acc_ref)
        process([(knt_ref[0], vnt_ref[0], lfn_ref[0])], True)

    process([(k_refs[i][0, 0].astype(BF16), v_refs[i][0, 0].astype(BF16), lf_refs[i][0, 0])
             for i in range(pps)], False)

    @pl.when(j == pl.num_programs(1) - 1)
    def _():
        o = acc_ref[...] * jnp.where(is_fox, 1.0 / l_ref[:, 0:1], 1.0)
        own = (lane // hd) == row_h
        ms = jnp.sum(jnp.where(own, o * o, 0.0), axis=-1, keepdims=True) * (1.0 / hd)
        on = jnp.where(own, o * lax.rsqrt(ms + RMS_EPS), 0.0)
        outs = [jnp.sum(on[t * nh:(t + 1) * nh], axis=0, keepdims=True) for t in range(n_tok)]
        o_ref[0] = (jnp.concatenate(outs, axis=0) * g_ref[...]).astype(BF16)


def _decode_attention(q, k_new_t, v_new_t, lf_new_t, cache_kt, cache_vt, cache_lf_t, page_table,
                      g_mix2, *, layer, nh, nf, hd):
    db, n_tok, mix = q.shape
    n_pages = page_table.shape[1]
    nfp, ps = cache_lf_t.shape[2:]
    pps = max(p for p in (16, 8, 4, 2, 1) if n_pages % p == 0)
    rows = n_tok * nh
    tri = _stack_rows((jnp.arange(ps)[:, None] > jnp.arange(ps)[None, :]).astype(BF16), 3)
    per_b = lambda b, j, pt: (b, 0, 0)
    const = lambda b, j, pt: (0, 0)

    def page_map(i):
        return lambda b, j, pt: (pt[b, n_pages - 1 - (j * pps + i)], layer, 0, 0)

    in_specs = [pl.BlockSpec((1, n_tok, mix), per_b),
                pl.BlockSpec((1, mix, ps), per_b),
                pl.BlockSpec((1, mix, ps), per_b),
                pl.BlockSpec((1, nfp, ps), per_b),
                pl.BlockSpec((3 * ps, ps), const),
                pl.BlockSpec((1, mix), const)]
    in_specs += [pl.BlockSpec((1, 1, mix, ps), page_map(i)) for i in range(pps)]
    in_specs += [pl.BlockSpec((1, 1, mix, ps), page_map(i)) for i in range(pps)]
    in_specs += [pl.BlockSpec((1, 1, nfp, ps), page_map(i)) for i in range(pps)]
    grid_spec = pltpu.PrefetchScalarGridSpec(
        num_scalar_prefetch=1,
        grid=(db, n_pages // pps),
        in_specs=in_specs,
        out_specs=pl.BlockSpec((1, n_tok, mix), per_b),
        scratch_shapes=[pltpu.VMEM((rows, mix), BF16),
                        pltpu.VMEM((rows, V7X_LANES), F32),
                        pltpu.VMEM((rows, V7X_LANES), F32),
                        pltpu.VMEM((rows, V7X_LANES), F32),
                        pltpu.VMEM((nfp, V7X_LANES), F32),
                        pltpu.VMEM((rows, mix), F32)])
    return pl.pallas_call(
        functools.partial(_decode_kernel, pps=pps, n_tok=n_tok, nh=nh, nf=nf, hd=hd),
        out_shape=jax.ShapeDtypeStruct((db, n_tok, mix), BF16),
        grid_spec=grid_spec,
        compiler_params=pltpu.CompilerParams(
            dimension_semantics=("arbitrary", "arbitrary"),
            vmem_limit_bytes=V7X_VMEM_BUDGET),
        name="decode_attn",
    )(page_table, q, k_new_t, v_new_t, lf_new_t, tri, g_mix2,
      *([cache_kt] * pps), *([cache_vt] * pps), *([cache_lf_t] * pps))


def _outproj_kernel(mix_ref, x_ref, g1_ref, sc_ref, sh_ref, w_ref, lg_ref, lb_ref, wr_ref, br_ref,
                    x1_ref, h2_ref, gate_ref, *, alpha, ng, epg):
    mo = _dot(mix_ref[0], w_ref[...])
    x1 = _layer_norm(alpha * x_ref[0] + g1_ref[0] * mo, lg_ref[...], lb_ref[...])
    x1_ref[0] = x1
    h2 = x1 * (1.0 + sc_ref[0]) + sh_ref[0]
    h2_ref[0] = h2.astype(BF16)
    h_hi, h_lo = _split2(h2)
    logits = _dot(jnp.concatenate([h_hi, h_lo, h_hi], axis=-1), wr_ref[...]) + br_ref[...]
    lane = lax.broadcasted_iota(jnp.int32, logits.shape, 1)
    big = jnp.int32(2 ** 30)
    neg = -jnp.inf
    gl = jnp.where(lane < ng, logits, neg)
    gmax = jnp.max(gl, axis=-1, keepdims=True)
    gidx = jnp.min(jnp.where(gl == gmax, lane, big), axis=-1, keepdims=True)
    g_w = 1.0 / jnp.sum(jnp.exp(gl - gmax), axis=-1, keepdims=True)
    in_group = (lane >= ng + gidx * epg) & (lane < ng + (gidx + 1) * epg)
    el = jnp.where(in_group, logits, neg)
    m1 = jnp.max(el, axis=-1, keepdims=True)
    i1 = jnp.min(jnp.where(el == m1, lane, big), axis=-1, keepdims=True)
    el2 = jnp.where(lane == i1, neg, el)
    m2 = jnp.max(el2, axis=-1, keepdims=True)
    i2 = jnp.min(jnp.where(el2 == m2, lane, big), axis=-1, keepdims=True)
    den = jnp.sum(jnp.exp(el - m1), axis=-1, keepdims=True)
    p1 = 1.0 / den
    p2 = jnp.exp(m2 - m1) / den
    w1 = p1 / (p1 + p2)
    w2 = p2 / (p1 + p2)
    gate_ref[0] = jnp.where(lane == i1, g_w * w1, jnp.where(lane == i2, g_w * w2, 0.0))


def _outproj(mixn, x, g1, sc2, sh2, w_out, ln_g, ln_b, w_r, b_r, *, alpha, ng, epg):
    nb, s, d = x.shape
    mix = mixn.shape[2]
    mrows = g1.shape[1]
    ts = min(s, 512)
    assert s % ts == 0
    tok = lambda b, i: (b, i, 0)
    mod = lambda b, i: (b, 0, 0)
    const = lambda b, i: (0, 0)
    return pl.pallas_call(
        functools.partial(_outproj_kernel, alpha=alpha, ng=ng, epg=epg),
        out_shape=(jax.ShapeDtypeStruct((nb, s, d), F32),
                   jax.ShapeDtypeStruct((nb, s, d), BF16),
                   jax.ShapeDtypeStruct((nb, s, V7X_LANES), F32)),
        grid=(nb, s // ts),
        in_specs=[pl.BlockSpec((1, ts, mix), tok),
                  pl.BlockSpec((1, ts, d), tok),
                  pl.BlockSpec((1, mrows, d), mod),
                  pl.BlockSpec((1, mrows, d), mod),
                  pl.BlockSpec((1, mrows, d), mod),
                  pl.BlockSpec((mix, d), const),
                  pl.BlockSpec((1, d), const),
                  pl.BlockSpec((1, d), const),
                  pl.BlockSpec((3 * d, V7X_LANES), const),
                  pl.BlockSpec((1, V7X_LANES), const)],
        out_specs=(pl.BlockSpec((1, ts, d), tok),
                   pl.BlockSpec((1, ts, d), tok),
                   pl.BlockSpec((1, ts, V7X_LANES), tok)),
        compiler_params=pltpu.CompilerParams(
            dimension_semantics=("arbitrary", "arbitrary"),
            vmem_limit_bytes=V7X_VMEM_BUDGET),
        name="outproj_router",
    )(mixn, x, g1, sc2, sh2, w_out, ln_g, ln_b, w_r, b_r)


def _moe_kernel(h_ref, gate_ref, x1_ref, g2_ref, wg_ref, wu_ref, wd_ref, lg_ref, lb_ref,
                y_ref, acc_ref, *, alpha, ng, ep):
    e = pl.program_id(2)

    @pl.when(e == 0)
    def _():
        acc_ref[...] = jnp.zeros_like(acc_ref)

    h = h_ref[0]
    gates = gate_ref[0]
    lane = lax.broadcasted_iota(jnp.int32, gates.shape, 1)
    zs = []
    for i in range(ep):
        a = _dot(h, wg_ref[i])
        u = _dot(h, wu_ref[i])
        gcol = jnp.sum(jnp.where(lane == ng + e * ep + i, gates, 0.0), axis=-1, keepdims=True)
        zs.append((a * (1.0 / (1.0 + jnp.exp(-a))) * u * gcol).astype(BF16))
    zz = zs[0] if ep == 1 else jnp.concatenate(zs, axis=-1)
    acc_ref[...] += _dot(zz, wd_ref[...])

    @pl.when(e == pl.num_programs(2) - 1)
    def _():
        y_ref[0] = _layer_norm(alpha * x1_ref[0] + g2_ref[0] * acc_ref[...],
                               lg_ref[...], lb_ref[...])


def _moe(h2, gates, x1, g2, w_gate, w_up, w_down2, ln_g, ln_b, *, alpha, ng):
    nb, s, d = x1.shape
    ne, _, f = w_gate.shape
    mrows = g2.shape[1]
    tm = min(s, 1024)
    assert s % tm == 0
    ep = 4 if ne % 4 == 0 else 1
    tok = lambda b, i, e: (b, i, 0)
    mod = lambda b, i, e: (b, 0, 0)
    const = lambda b, i, e: (0, 0)
    return pl.pallas_call(
        functools.partial(_moe_kernel, alpha=alpha, ng=ng, ep=ep),
        out_shape=jax.ShapeDtypeStruct((nb, s, d), F32),
        grid=(nb, s // tm, ne // ep),
        in_specs=[pl.BlockSpec((1, tm, d), tok),
                  pl.BlockSpec((1, tm, V7X_LANES), tok),
                  pl.BlockSpec((1, tm, d), tok),
                  pl.BlockSpec((1, mrows, d), mod),
                  pl.BlockSpec((ep, d, f), lambda b, i, e: (e, 0, 0)),
                  pl.BlockSpec((ep, d, f), lambda b, i, e: (e, 0, 0)),
                  pl.BlockSpec((ep * f, d), lambda b, i, e: (e, 0)),
                  pl.BlockSpec((1, d), const),
                  pl.BlockSpec((1, d), const)],
        out_specs=pl.BlockSpec((1, tm, d), tok),
        scratch_shapes=[pltpu.VMEM((tm, d), F32)],
        compiler_params=pltpu.CompilerParams(
            dimension_semantics=("arbitrary", "arbitrary", "arbitrary"),
            vmem_limit_bytes=V7X_VMEM_BUDGET),
        name="moe_experts",
    )(h2, gates, x1, g2, w_gate, w_up, w_down2, ln_g, ln_b)


def _pad_to(a, axis, size):
    if a.shape[axis] == size:
        return a
    pad = [(0, 0)] * a.ndim
    pad[axis] = (0, size - a.shape[axis])
    return jnp.pad(a, pad)


def kernel(x_prompt, x_sample, c_prompt, c_sample, cache_k, cache_v, cache_logf, page_table, w_in, b_forget, g_mix, w_out, w_ada, b_ada, ln1_g, ln1_b, ln2_g, ln2_b, w_router_group, b_router_group, w_router_expert, b_router_expert, w_gate_e, w_up_e, w_down_e):
    b, s, d = x_prompt.shape
    db, n_tok, _ = x_sample.shape
    depth = w_in.shape[0]
    n_pool, _, ps, nh, hd = cache_k.shape
    nf = b_forget.shape[1]
    mix = nh * hd
    ng = w_router_group.shape[2]
    ne = w_router_expert.shape[2]
    epg = ne // ng
    f = w_gate_e.shape[3]
    nfp = _round_up(nf, V7X_SUBLANES)
    alpha = float((2 * depth) ** 0.25)
    assert nf % 2 == 0 and (nh - nf) % 2 == 0 and 2 * hd == V7X_LANES
    assert ng + ne <= V7X_LANES and n_tok <= ps

    cache_kt = jnp.transpose(cache_k, (0, 1, 3, 4, 2)).reshape(n_pool, depth, mix, ps)
    cache_vt = jnp.transpose(cache_v, (0, 1, 3, 4, 2)).reshape(n_pool, depth, mix, ps)
    cache_lf_t = _pad_to(jnp.swapaxes(cache_logf, 2, 3), 2, nfp)

    n_c = b + db
    c_all = _pad_to(jnp.concatenate([c_prompt, c_sample], axis=0), 0, _round_up(n_c, V7X_SUBLANES))

    yp, ys = x_prompt, x_sample.reshape(1, db * n_tok, d)
    outs = {name: [] for name in ("kp", "vp", "lp", "ks", "vs", "ls")}
    for l in range(depth):
        mods = _adaln(c_all, w_ada[l], b_ada[l])
        mods_p = [m.reshape(b, 1, d) for m in jnp.split(mods[:b], 6, axis=-1)]
        mods_s = [jnp.repeat(m, n_tok, axis=0).reshape(1, db * n_tok, d)
                  for m in jnp.split(mods[b:n_c], 6, axis=-1)]

        w_qkv = w_in[l][:, :3 * mix].astype(BF16)
        w_f = w_in[l][:, 3 * mix:]
        wf = _pad_to(w_f, 1, V7X_LANES).astype(BF16)
        wft = _pad_to(w_f.T, 0, nfp).astype(BF16)
        bf = b_forget[l].reshape(1, nf)
        bft = _pad_to(b_forget[l].reshape(nf, 1), 0, nfp)
        g_mix2 = g_mix[l].reshape(1, mix)
        w_out_b = w_out[l].astype(BF16)
        w_r = _pad_to(jnp.concatenate([w_router_group[l], w_router_expert[l]], axis=1), 1, V7X_LANES)
        w_r_hi, w_r_lo = _split2(w_r)
        w_r = jnp.concatenate([w_r_hi, w_r_hi, w_r_lo], axis=0)
        b_r = _pad_to(jnp.concatenate([b_router_group[l], b_router_expert[l]]).reshape(1, ng + ne),
                      1, V7X_LANES)
        wg_b = w_gate_e[l].astype(BF16)
        wu_b = w_up_e[l].astype(BF16)
        wd_b = w_down_e[l].astype(BF16).reshape(ne * f, d)
        lg1, lb1 = ln1_g[l].reshape(1, d), ln1_b[l].reshape(1, d)
        lg2, lb2 = ln2_g[l].reshape(1, d), ln2_b[l].reshape(1, d)
        proj = functools.partial(_inproj, w_qkv=w_qkv, wf=wf, wft=wft, bf=bf, bft=bft,
                                 mix=mix, nf=nf, hd=hd)
        ffn = functools.partial(_moe, w_gate=wg_b, w_up=wu_b, w_down2=wd_b, ln_g=lg2, ln_b=lb2,
                                alpha=alpha, ng=ng)
        mixer_out = functools.partial(_outproj, w_out=w_out_b, ln_g=lg1, ln_b=lb1, w_r=w_r, b_r=b_r,
                                      alpha=alpha, ng=ng, epg=epg)

        sh1, sc1, g1, sh2, sc2, g2 = mods_p
        q, k, v, kb, vb, lf, cum = proj(yp, sc1, sh1)
        cum4 = cum[:, :nf].reshape(b, nf // 2, 2, s)
        mixn = _prompt_attention(q, kb, vb, cum4, g_mix2, nf=nf, hd=hd)
        x1, h2, gates = mixer_out(mixn, yp, g1, sc2, sh2)
        yp = ffn(h2, gates, x1, g2)
        outs["kp"].append(k.reshape(b, s, nh, hd))
        outs["vp"].append(v.reshape(b, s, nh, hd))
        outs["lp"].append(lf)

        sh1, sc1, g1, sh2, sc2, g2 = mods_s
        q, k, v, kb, vb, lf, _ = proj(ys, sc1, sh1)
        q_s = q.reshape(db, n_tok, mix).astype(F32)
        k_new_t = _pad_to(jnp.swapaxes(kb.reshape(db, n_tok, mix), 1, 2), 2, ps)
        v_new_t = _pad_to(jnp.swapaxes(vb.reshape(db, n_tok, mix), 1, 2), 2, ps)
        lf_new_t = _pad_to(_pad_to(jnp.swapaxes(lf.reshape(db, n_tok, nf), 1, 2), 1, nfp), 2, ps)
        mixn = _decode_attention(q_s, k_new_t, v_new_t, lf_new_t, cache_kt, cache_vt, cache_lf_t,
                                 page_table, g_mix2, layer=l, nh=nh, nf=nf, hd=hd)
        x1, h2, gates = mixer_out(mixn.reshape(1, db * n_tok, mix), ys, g1, sc2, sh2)
        ys = ffn(h2, gates, x1, g2)
        outs["ks"].append(k.reshape(db, n_tok, nh, hd))
        outs["vs"].append(v.reshape(db, n_tok, nh, hd))
        outs["ls"].append(lf.reshape(db, n_tok, nf))

    stack = lambda name: jnp.stack(outs[name], axis=1)
    return (yp, ys.reshape(db, n_tok, d), stack("kp"), stack("vp"), stack("lp"),
            stack("ks"), stack("vs"), stack("ls"))
```

```python
import functools

import jax
import jax.numpy as jnp
from jax import lax
from jax.experimental import pallas as pl
from jax.experimental.pallas import tpu as pltpu

F32 = jnp.float32
BF16 = jnp.bfloat16
LN_EPS = 1e-5
RMS_EPS = 1e-6
LOG2E = 1.4426950408889634
SB_DEAD_MASS = 104.0

V7X_LANES = 128
V7X_SUBLANES = 8
V7X_MXU_DIM = 256
V7X_VMEM_BUDGET = 56 * 1024 * 1024

_NT = (((1,), (1,)), ((), ()))


def _dot(a, b):
    return jnp.dot(a, b, preferred_element_type=F32)


def _dot_nt(a, b):
    return lax.dot_general(a, b, _NT, preferred_element_type=F32)


def _round_up(n, m):
    return (n + m - 1) // m * m


def _split3(x):
    hi = x.astype(BF16)
    r1 = x - hi.astype(F32)
    mid = r1.astype(BF16)
    lo = (r1 - mid.astype(F32)).astype(BF16)
    return hi, mid, lo


def _split2(x):
    hi = x.astype(BF16)
    lo = (x - hi.astype(F32)).astype(BF16)
    return hi, lo


def _dot_parts(parts, tri_stack):
    width = tri_stack.shape[1]
    return _dot(jnp.concatenate(parts, axis=-1), tri_stack[0:len(parts) * width])


def _stack_rows(tri, n):
    return jnp.concatenate([tri] * n, axis=0)


def _softplus_neg_abs(z):
    return jnp.log(1.0 + jnp.exp2(jnp.abs(z) * (-LOG2E)))


def _layer_norm(x, g, b):
    mu = jnp.mean(x, axis=-1, keepdims=True)
    xc = x - mu
    var = jnp.mean(xc * xc, axis=-1, keepdims=True)
    return xc * lax.rsqrt(var + LN_EPS) * g + b


def _adaln_kernel(c_ref, w_ref, b_ref, o_ref):
    c = c_ref[...]
    sc = c * (1.0 / (1.0 + jnp.exp(-c)))
    o_ref[...] = jnp.dot(sc, w_ref[...], precision=lax.Precision.HIGHEST,
                         preferred_element_type=F32) + b_ref[...]


def _adaln(c_all, w_ada, b_ada):
    rows, d = c_all.shape
    n = w_ada.shape[1]
    tn = min(n, 512)
    return pl.pallas_call(
        _adaln_kernel,
        out_shape=jax.ShapeDtypeStruct((rows, n), F32),
        grid=(n // tn,),
        in_specs=[pl.BlockSpec((rows, d), lambda j: (0, 0)),
                  pl.BlockSpec((d, tn), lambda j: (0, j)),
                  pl.BlockSpec((1, tn), lambda j: (0, j))],
        out_specs=pl.BlockSpec((rows, tn), lambda j: (0, j)),
        compiler_params=pltpu.CompilerParams(dimension_semantics=("arbitrary",)),
        name="adaln",
    )(c_all, w_ada, b_ada.reshape(1, n))


def _inproj_kernel(x_ref, sc_ref, sh_ref, w_ref, wf_ref, wft_ref, bf_ref, bft_ref, tri_ref, qs_ref,
                   q_ref, k_ref, v_ref, kb_ref, vb_ref, lf_ref, cum_ref, carry_ref,
                   *, mix, nf):
    si = pl.program_id(1)
    h = (x_ref[0] * (1.0 + sc_ref[0]) + sh_ref[0]).astype(BF16)
    q_ref[0] = (_dot(h, w_ref[:, 0:mix]) * qs_ref[...]).astype(BF16)
    k = _dot(h, w_ref[:, mix:2 * mix])
    k_ref[0] = k
    kb_ref[0] = k.astype(BF16)
    v = _dot(h, w_ref[:, 2 * mix:3 * mix])
    v_ref[0] = v
    vb_ref[0] = v.astype(BF16)

    logit = _dot(h, wf_ref[...])[:, 0:nf] + bf_ref[...]
    lf_ref[0] = jnp.minimum(logit, 0.0) - _softplus_neg_abs(logit)

    logit_t = _dot_nt(wft_ref[...], h) + bft_ref[...]
    lf_t = jnp.minimum(logit_t, 0.0) - _softplus_neg_abs(logit_t)
    local = _dot_parts(_split3(lf_t), tri_ref[...])

    @pl.when(si == 0)
    def _():
        carry_ref[...] = jnp.zeros_like(carry_ref)

    cum = local + carry_ref[:, 0:1]
    cum_ref[0] = cum
    ts = cum.shape[1]
    carry_ref[...] = jnp.broadcast_to(cum[:, ts - 1:ts], carry_ref.shape)


def _inproj(x, sc, sh, w_qkv, wf, wft, bf, bft, *, mix, nf, hd):
    nb, s, d = x.shape
    mrows = sc.shape[1]
    ts = min(s, 512)
    assert s % ts == 0
    nfp = wft.shape[0]
    tri = _stack_rows((jnp.arange(ts)[:, None] <= jnp.arange(ts)[None, :]).astype(BF16), 3)
    col_head = jnp.arange(mix) // hd
    q_scale = (jnp.where(col_head < nf, LOG2E, 1.0) * float(hd) ** -0.5).astype(F32).reshape(1, mix)
    tok = lambda b, i: (b, i, 0)
    mod = lambda b, i: (b, 0, 0)
    const = lambda b, i: (0, 0)
    return pl.pallas_call(
        functools.partial(_inproj_kernel, mix=mix, nf=nf),
        out_shape=(jax.ShapeDtypeStruct((nb, s, mix), BF16),
                   jax.ShapeDtypeStruct((nb, s, mix), F32),
                   jax.ShapeDtypeStruct((nb, s, mix), F32),
                   jax.ShapeDtypeStruct((nb, s, mix), BF16),
                   jax.ShapeDtypeStruct((nb, s, mix), BF16),
                   jax.ShapeDtypeStruct((nb, s, nf), F32),
                   jax.ShapeDtypeStruct((nb, nfp, s), F32)),
        grid=(nb, s // ts),
        in_specs=[pl.BlockSpec((1, ts, d), tok),
                  pl.BlockSpec((1, mrows, d), mod),
                  pl.BlockSpec((1, mrows, d), mod),
                  pl.BlockSpec((d, 3 * mix), const),
                  pl.BlockSpec((d, V7X_LANES), const),
                  pl.BlockSpec((nfp, d), const),
                  pl.BlockSpec((1, nf), const),
                  pl.BlockSpec((nfp, 1), const),
                  pl.BlockSpec((3 * ts, ts), const),
                  pl.BlockSpec((1, mix), const)],
        out_specs=(pl.BlockSpec((1, ts, mix), tok),
                   pl.BlockSpec((1, ts, mix), tok),
                   pl.BlockSpec((1, ts, mix), tok),
                   pl.BlockSpec((1, ts, mix), tok),
                   pl.BlockSpec((1, ts, mix), tok),
                   pl.BlockSpec((1, ts, nf), tok),
                   pl.BlockSpec((1, nfp, ts), lambda b, i: (b, 0, i))),
        scratch_shapes=[pltpu.VMEM((nfp, V7X_LANES), F32)],
        compiler_params=pltpu.CompilerParams(
            dimension_semantics=("arbitrary", "arbitrary"),
            vmem_limit_bytes=V7X_VMEM_BUDGET),
        name="inproj",
    )(x, sc, sh, w_qkv, wf, wft, bf, bft, tri, q_scale)


def _pair_rmsnorm(outs, lane, hd, g):
    first = (lane // hd) == 0
    o = jnp.where(first, outs[0], outs[1])
    osq = o * o
    ms0 = jnp.sum(jnp.where(first, osq, 0.0), axis=-1, keepdims=True)
    ms1 = jnp.sum(jnp.where(first, 0.0, osq), axis=-1, keepdims=True)
    ms = jnp.where(first, ms0, ms1) * (1.0 / hd)
    return o * lax.rsqrt(ms + RMS_EPS) * g


def _fox_kernel(q_ref, k_ref, v_ref, cum_ref, g_ref, o_ref, *, tq, hd):
    qi = pl.program_id(2)
    q = q_ref[0]
    lane = lax.broadcasted_iota(jnp.int32, (1, V7X_LANES), 1)
    row = lax.broadcasted_iota(jnp.int32, (tq, tq), 0)
    col = lax.broadcasted_iota(jnp.int32, (tq, tq), 1)
    qh = [jnp.where((lane // hd) == h, q, jnp.zeros_like(q)) for h in range(2)]

    def step(kb, carry, masked):
        k0 = pl.multiple_of(kb * tq, tq)
        k = k_ref[0, pl.ds(k0, tq), :]
        v = v_ref[0, pl.ds(k0, tq), :]
        new = []
        for h in range(2):
            m, l, acc = carry[h]
            s = _dot_nt(qh[h], k) - cum_ref[0, 0, h:h + 1, pl.ds(k0, tq)] * LOG2E
            if masked:
                s = jnp.where(col <= row, s, -jnp.inf)
            m_new = jnp.maximum(m, jnp.max(s, axis=-1, keepdims=True))
            alpha = jnp.exp2(m - m_new)
            p = jnp.exp2(s - m_new)
            l = alpha * l + jnp.sum(p, axis=-1, keepdims=True)
            acc = alpha * acc + _dot(p.astype(BF16), v)
            new.append((m_new, l, acc))
        return tuple(new)

    init = tuple((jnp.full((tq, 1), -jnp.inf, F32), jnp.zeros((tq, 1), F32),
                  jnp.zeros((tq, V7X_LANES), F32)) for _ in range(2))
    carry = lax.fori_loop(0, qi, functools.partial(step, masked=False), init)
    carry = step(qi, carry, True)
    outs = [acc / l for (_, l, acc) in carry]
    o_ref[0] = _pair_rmsnorm(outs, lane, hd, g_ref[...]).astype(BF16)


def _sb_kernel(q_ref, k_ref, v_ref, tri_ref, g_ref, o_ref, *, tq, sub, hd):
    qi = pl.program_id(2)
    q = q_ref[0]
    lane = lax.broadcasted_iota(jnp.int32, (1, V7X_LANES), 1)
    row = lax.broadcasted_iota(jnp.int32, (tq, tq), 0)
    col = lax.broadcasted_iota(jnp.int32, (tq, tq), 1)
    qh = [jnp.where((lane // hd) == h, q, jnp.zeros_like(q)) for h in range(2)]
    n_sub = tq // sub

    def step(kb, carry, qs, masked):
        k0 = pl.multiple_of(kb * tq, tq)
        k = k_ref[0, pl.ds(k0, tq), :]
        v = v_ref[0, pl.ds(k0, tq), :]
        tri = tri_ref[...]
        new = []
        for h in range(2):
            rsum, acc = carry[h]
            z = _dot_nt(qs[h], k)
            sp = jnp.maximum(z, 0.0) + _softplus_neg_abs(z)
            lsz = z - sp
            if masked:
                valid = col < row
                sp = jnp.where(valid, sp, 0.0)
            afters = []
            later = rsum
            for c in reversed(range(n_sub)):
                part = sp[:, c * sub:(c + 1) * sub]
                afters.append(_dot(part.astype(BF16), tri) + later)
                later = later + jnp.sum(part, axis=-1, keepdims=True)
            after = afters[0] if n_sub == 1 else jnp.concatenate(afters[::-1], axis=-1)
            a = jnp.exp(lsz - after)
            if masked:
                a = jnp.where(valid, a, 0.0)
            new.append((later, acc + _dot(a.astype(BF16), v)))
        return tuple(new)

    def least_mass(carry, first_row=0):
        return jnp.minimum(jnp.min(carry[0][0][first_row:]), jnp.min(carry[1][0][first_row:]))

    def walk_earlier(start, carry, qs, first_row=0):
        def more_to_do(state):
            i, _, least = state
            return jnp.logical_and(i < qi, least < SB_DEAD_MASS)

        def earlier_block(state):
            i, c, _ = state
            c = step(qi - 1 - i, c, qs, False)
            return i + 1, c, least_mass(c, first_row)

        i, carry, _ = lax.while_loop(more_to_do, earlier_block, (start, carry, least_mass(carry, first_row)))
        return i, carry

    init = tuple((jnp.zeros((tq, 1), F32), jnp.zeros((tq, V7X_LANES), F32)) for _ in range(2))
    carry = step(qi, init, qh, True)
    top = tq // 2
    i, carry = walk_earlier(jnp.int32(0), carry, qh, first_row=top)
    _, upper = walk_earlier(i, tuple((r[:top], a[:top]) for (r, a) in carry), [x[:top] for x in qh])
    carry = tuple((r, jnp.concatenate([ua, a[top:]], axis=0)) for (_, ua), (r, a) in zip(upper, carry))
    outs = [acc for (_, acc) in carry]
    o_ref[0] = _pair_rmsnorm(outs, lane, hd, g_ref[...]).astype(BF16)


def _prompt_attention(q, kb, vb, cum4, g_mix2, *, nf, hd):
    b, s, mix = q.shape
    nh = mix // hd
    fox_pairs = nf // 2
    sb_pairs = (nh - nf) // 2
    tq_fox = min(s, 1024)
    tq_sb = min(s, 512)
    sub = min(tq_sb, V7X_MXU_DIM)
    assert s % tq_fox == 0 and s % tq_sb == 0 and tq_sb % sub == 0
    tri = (jnp.arange(sub)[:, None] > jnp.arange(sub)[None, :]).astype(BF16)
    params = pltpu.CompilerParams(
        dimension_semantics=("arbitrary", "arbitrary", "arbitrary"),
        vmem_limit_bytes=V7X_VMEM_BUDGET)

    def specs(off, tq):
        return (pl.BlockSpec((1, tq, V7X_LANES), lambda bi, p, i: (bi, i, p + off)),
                pl.BlockSpec((1, s, V7X_LANES), lambda bi, p, i: (bi, 0, p + off)),
                pl.BlockSpec((1, s, V7X_LANES), lambda bi, p, i: (bi, 0, p + off)),
                pl.BlockSpec((1, V7X_LANES), lambda bi, p, i: (0, p + off)),
                pl.BlockSpec((1, tq, V7X_LANES), lambda bi, p, i: (bi, i, p)))

    qs, ks, vs, gs, os_ = specs(0, tq_fox)
    o_fox = pl.pallas_call(
        functools.partial(_fox_kernel, tq=tq_fox, hd=hd),
        out_shape=jax.ShapeDtypeStruct((b, s, fox_pairs * V7X_LANES), BF16),
        grid=(b, fox_pairs, s // tq_fox),
        in_specs=[qs, ks, vs,
                  pl.BlockSpec((1, 1, 2, s), lambda bi, p, i: (bi, p, 0, 0)),
                  gs],
        out_specs=os_,
        compiler_params=params,
        name="fox_prompt",
    )(q, kb, vb, cum4, g_mix2)
    qs, ks, vs, gs, os_ = specs(fox_pairs, tq_sb)
    o_sb = pl.pallas_call(
        functools.partial(_sb_kernel, tq=tq_sb, sub=sub, hd=hd),
        out_shape=jax.ShapeDtypeStruct((b, s, sb_pairs * V7X_LANES), BF16),
        grid=(b, sb_pairs, s // tq_sb),
        in_specs=[qs, ks, vs, pl.BlockSpec((sub, sub), lambda bi, p, i: (0, 0)), gs],
        out_specs=os_,
        compiler_params=params,
        name="sb_prompt",
    )(q, kb, vb, tri, g_mix2)
    return o_fox, o_sb


def _decode_kernel(pt_ref, q_ref, knt_ref, vnt_ref, lfn_ref, tri_ref, g_ref, *rest,
                   pps, n_tok, nh, nf, hd):
    del pt_ref
    k_refs = rest[0:pps]
    v_refs = rest[pps:2 * pps]
    lf_refs = rest[2 * pps:3 * pps]
    o_ref = rest[3 * pps]
    qbd_ref, m_ref, l_ref, r_ref, gc_ref, acc_ref = rest[3 * pps + 1:]
    j = pl.program_id(1)
    rows = n_tok * nh
    ps = tri_ref.shape[1]
    mix = nh * hd
    rowi = lax.broadcasted_iota(jnp.int32, (rows, 1), 0)
    row_h = rowi % nh
    row_t = rowi // nh
    is_fox = row_h < nf
    key = lax.broadcasted_iota(jnp.int32, (1, ps), 1)
    lane = lax.broadcasted_iota(jnp.int32, (1, mix), 1)

    def process(blocks, new):
        tri = tri_ref[...]
        qbd = qbd_ref[...]
        g_off = gc_ref[0:nf, 0:1]
        r_off = r_ref[:, 0:1]
        xs, a_sbs = [], []
        for k_t, _, lf_t in blocks:
            s = _dot(qbd, k_t)
            lf = lf_t[0:nf]
            g_suffix = (_dot_parts(_split3(lf), tri) + g_off) * LOG2E
            g_off = g_off + jnp.sum(lf, axis=-1, keepdims=True)
            if nh > nf:
                g_suffix = jnp.concatenate([g_suffix, jnp.zeros((nh - nf, ps), F32)], axis=0)
            x = s + jnp.concatenate([g_suffix] * n_tok, axis=0)
            if new:
                x = jnp.where((key < n_tok) & (key <= row_t), x, -jnp.inf)
            xs.append(x)
            sp = jnp.maximum(s, 0.0) + _softplus_neg_abs(s)
            lsz = s - sp
            if new:
                valid = key < row_t
                sp = jnp.where(valid, sp, 0.0)
            a_sb = jnp.exp(lsz - (_dot(sp.astype(BF16), tri[0:ps]) + r_off))
            if new:
                a_sb = jnp.where(valid, a_sb, 0.0)
            a_sbs.append(a_sb)
            r_off = r_off + jnp.sum(sp, axis=-1, keepdims=True)
        gc_ref[0:nf, :] = jnp.broadcast_to(g_off, (nf, gc_ref.shape[1]))
        r_ref[...] = jnp.broadcast_to(r_off, r_ref.shape)
        m_prev = m_ref[:, 0:1]
        m_new = m_prev
        for x in xs:
            m_new = jnp.maximum(m_new, jnp.max(x, axis=-1, keepdims=True))
        alpha = jnp.exp2(m_prev - m_new)
        l_new = alpha * l_ref[:, 0:1]
        pv = None
        for x, a_sb, (_, v_t, _) in zip(xs, a_sbs, blocks):
            p_fox = jnp.exp2(x - m_new)
            l_new = l_new + jnp.sum(p_fox, axis=-1, keepdims=True)
            d = _dot_nt(jnp.where(is_fox, p_fox, a_sb).astype(BF16), v_t)
            pv = d if pv is None else pv + d
        m_ref[...] = jnp.broadcast_to(m_new, m_ref.shape)
        l_ref[...] = jnp.broadcast_to(l_new, l_ref.shape)
        acc_ref[...] = acc_ref[...] * jnp.where(is_fox, alpha, 1.0) + pv

    @pl.when(j == 0)
    def _():
        q = q_ref[0]
        head_rows = lax.broadcasted_iota(jnp.int32, (nh, 1), 0)
        own = (lane // hd) == head_rows
        qbd_ref[...] = jnp.concatenate(
            [jnp.where(own, jnp.broadcast_to(q[t:t + 1], (nh, mix)), 0.0) for t in range(n_tok)],
            axis=0).astype(BF16)
        m_ref[...] = jnp.full(m_ref.shape, -jnp.inf, F32)
        l_ref[...] = jnp.zeros_like(l_ref)
        r_ref[...] = jnp.zeros_like(r_ref)
        gc_ref[...] = jnp.zeros_like(gc_ref)
        acc_ref[...] = jnp.zeros_like(acc_ref)
        process([(knt_ref[0], vnt_ref[0], lfn_ref[0])], True)

    process([(k_refs[i][0, 0].astype(BF16), v_refs[i][0, 0].astype(BF16), lf_refs[i][0, 0])
             for i in range(pps)], False)

    @pl.when(j == pl.num_programs(1) - 1)
    def _():
        o = acc_ref[...] * jnp.where(is_fox, 1.0 / l_ref[:, 0:1], 1.0)
        own = (lane // hd) == row_h
        ms = jnp.sum(jnp.where(own, o * o, 0.0), axis=-1, keepdims=True) * (1.0 / hd)
        on = jnp.where(own, o * lax.rsqrt(ms + RMS_EPS), 0.0)
        outs = [jnp.sum(on[t * nh:(t + 1) * nh], axis=0, keepdims=True) for t in range(n_tok)]
        o_ref[0] = (jnp.concatenate(outs, axis=0) * g_ref[...]).astype(BF16)


def _decode_attention(q, k_new_t, v_new_t, lf_new_t, cache_kt, cache_vt, cache_lf_t, page_table,
                      g_mix2, *, layer, nh, nf, hd):
    db, n_tok, mix = q.shape
    n_pages = page_table.shape[1]
    nfp, ps = cache_lf_t.shape[2:]
    pps = max(p for p in (16, 8, 4, 2, 1) if n_pages % p == 0)
    rows = n_tok * nh
    tri = _stack_rows((jnp.arange(ps)[:, None] > jnp.arange(ps)[None, :]).astype(BF16), 3)
    per_b = lambda b, j, pt: (b, 0, 0)
    const = lambda b, j, pt: (0, 0)

    def page_map(i):
        return lambda b, j, pt: (pt[b, n_pages - 1 - (j * pps + i)], layer, 0, 0)

    in_specs = [pl.BlockSpec((1, n_tok, mix), per_b),
                pl.BlockSpec((1, mix, ps), per_b),
                pl.BlockSpec((1, mix, ps), per_b),
                pl.BlockSpec((1, nfp, ps), per_b),
                pl.BlockSpec((3 * ps, ps), const),
                pl.BlockSpec((1, mix), const)]
    in_specs += [pl.BlockSpec((1, 1, mix, ps), page_map(i)) for i in range(pps)]
    in_specs += [pl.BlockSpec((1, 1, mix, ps), page_map(i)) for i in range(pps)]
    in_specs += [pl.BlockSpec((1, 1, nfp, ps), page_map(i)) for i in range(pps)]
    grid_spec = pltpu.PrefetchScalarGridSpec(
        num_scalar_prefetch=1,
        grid=(db, n_pages // pps),
        in_specs=in_specs,
        out_specs=pl.BlockSpec((1, n_tok, mix), per_b),
        scratch_shapes=[pltpu.VMEM((rows, mix), BF16),
                        pltpu.VMEM((rows, V7X_LANES), F32),
                        pltpu.VMEM((rows, V7X_LANES), F32),
                        pltpu.VMEM((rows, V7X_LANES), F32),
                        pltpu.VMEM((nfp, V7X_LANES), F32),
                        pltpu.VMEM((rows, mix), F32)])
    return pl.pallas_call(
        functools.partial(_decode_kernel, pps=pps, n_tok=n_tok, nh=nh, nf=nf, hd=hd),
        out_shape=jax.ShapeDtypeStruct((db, n_tok, mix), BF16),
        grid_spec=grid_spec,
        compiler_params=pltpu.CompilerParams(
            dimension_semantics=("arbitrary", "arbitrary"),
            vmem_limit_bytes=V7X_VMEM_BUDGET),
        name="decode_attn",
    )(page_table, q, k_new_t, v_new_t, lf_new_t, tri, g_mix2,
      *([cache_kt] * pps), *([cache_vt] * pps), *([cache_lf_t] * pps))


def _outproj_kernel(*refs, n_parts, alpha, ng, epg):
    mix_refs = refs[:n_parts]
    (x_ref, g1_ref, sc_ref, sh_ref, w_ref, lg_ref, lb_ref, wr_ref, br_ref,
     x1_ref, h2_ref, gate_ref) = refs[n_parts:]
    mo, off = None, 0
    for part in mix_refs:
        width = part.shape[2]
        term = _dot(part[0], w_ref[off:off + width, :])
        mo = term if mo is None else mo + term
        off += width
    x1 = _layer_norm(alpha * x_ref[0] + g1_ref[0] * mo, lg_ref[...], lb_ref[...])
    x1_ref[0] = x1
    h2 = x1 * (1.0 + sc_ref[0]) + sh_ref[0]
    h2_ref[0] = h2.astype(BF16)
    h_hi, h_lo = _split2(h2)
    logits = _dot(jnp.concatenate([h_hi, h_lo, h_hi], axis=-1), wr_ref[...]) + br_ref[...]
    lane = lax.broadcasted_iota(jnp.int32, logits.shape, 1)
    big = jnp.int32(2 ** 30)
    neg = -jnp.inf
    gl = jnp.where(lane < ng, logits, neg)
    gmax = jnp.max(gl, axis=-1, keepdims=True)
    gidx = jnp.min(jnp.where(gl == gmax, lane, big), axis=-1, keepdims=True)
    g_w = 1.0 / jnp.sum(jnp.exp(gl - gmax), axis=-1, keepdims=True)
    in_group = (lane >= ng + gidx * epg) & (lane < ng + (gidx + 1) * epg)
    el = jnp.where(in_group, logits, neg)
    m1 = jnp.max(el, axis=-1, keepdims=True)
    i1 = jnp.min(jnp.where(el == m1, lane, big), axis=-1, keepdims=True)
    el2 = jnp.where(lane == i1, neg, el)
    m2 = jnp.max(el2, axis=-1, keepdims=True)
    i2 = jnp.min(jnp.where(el2 == m2, lane, big), axis=-1, keepdims=True)
    den = jnp.sum(jnp.exp(el - m1), axis=-1, keepdims=True)
    p1 = 1.0 / den
    p2 = jnp.exp(m2 - m1) / den
    w1 = p1 / (p1 + p2)
    w2 = p2 / (p1 + p2)
    gate_ref[0] = jnp.where(lane == i1, g_w * w1, jnp.where(lane == i2, g_w * w2, 0.0))


def _outproj(mix_parts, x, g1, sc2, sh2, w_out, ln_g, ln_b, w_r, b_r, *, alpha, ng, epg):
    nb, s, d = x.shape
    mix = sum(part.shape[2] for part in mix_parts)
    mrows = g1.shape[1]
    ts = min(s, 512)
    assert s % ts == 0
    tok = lambda b, i: (b, i, 0)
    mod = lambda b, i: (b, 0, 0)
    const = lambda b, i: (0, 0)
    return pl.pallas_call(
        functools.partial(_outproj_kernel, n_parts=len(mix_parts), alpha=alpha, ng=ng, epg=epg),
        out_shape=(jax.ShapeDtypeStruct((nb, s, d), F32),
                   jax.ShapeDtypeStruct((nb, s, d), BF16),
                   jax.ShapeDtypeStruct((nb, s, V7X_LANES), F32)),
        grid=(nb, s // ts),
        in_specs=[pl.BlockSpec((1, ts, part.shape[2]), tok) for part in mix_parts] + [
                  pl.BlockSpec((1, ts, d), tok),
                  pl.BlockSpec((1, mrows, d), mod),
                  pl.BlockSpec((1, mrows, d), mod),
                  pl.BlockSpec((1, mrows, d), mod),
                  pl.BlockSpec((mix, d), const),
                  pl.BlockSpec((1, d), const),
                  pl.BlockSpec((1, d), const),
                  pl.BlockSpec((3 * d, V7X_LANES), const),
                  pl.BlockSpec((1, V7X_LANES), const)],
        out_specs=(pl.BlockSpec((1, ts, d), tok),
                   pl.BlockSpec((1, ts, d), tok),
                   pl.BlockSpec((1, ts, V7X_LANES), tok)),
        compiler_params=pltpu.CompilerParams(
            dimension_semantics=("arbitrary", "arbitrary"),
            vmem_limit_bytes=V7X_VMEM_BUDGET),
        name="outproj_router",
    )(*mix_parts, x, g1, sc2, sh2, w_out, ln_g, ln_b, w_r, b_r)


def _moe_kernel(h_ref, gate_ref, x1_ref, g2_ref, wg_ref, wu_ref, wd_ref, lg_ref, lb_ref,
                y_ref, acc_ref, *, alpha, ng, ep):
    e = pl.program_id(2)

    @pl.when(e == 0)
    def _():
        acc_ref[...] = jnp.zeros_like(acc_ref)

    h = h_ref[0]
    gates = gate_ref[0]
    lane = lax.broadcasted_iota(jnp.int32, gates.shape, 1)
    zs = []
    for i in range(ep):
        a = _dot(h, wg_ref[i])
        u = _dot(h, wu_ref[i])
        gcol = jnp.sum(jnp.where(lane == ng + e * ep + i, gates, 0.0), axis=-1, keepdims=True)
        zs.append((a * (1.0 / (1.0 + jnp.exp(-a))) * u * gcol).astype(BF16))
    zz = zs[0] if ep == 1 else jnp.concatenate(zs, axis=-1)
    acc_ref[...] += _dot(zz, wd_ref[...])

    @pl.when(e == pl.num_programs(2) - 1)
    def _():
        y_ref[0] = _layer_norm(alpha * x1_ref[0] + g2_ref[0] * acc_ref[...],
                               lg_ref[...], lb_ref[...])


def _moe(h2, gates, x1, g2, w_gate, w_up, w_down2, ln_g, ln_b, *, alpha, ng):
    nb, s, d = x1.shape
    ne, _, f = w_gate.shape
    mrows = g2.shape[1]
    tm = min(s, 1024)
    assert s % tm == 0
    ep = 4 if ne % 4 == 0 else 1
    tok = lambda b, i, e: (b, i, 0)
    mod = lambda b, i, e: (b, 0, 0)
    const = lambda b, i, e: (0, 0)
    return pl.pallas_call(
        functools.partial(_moe_kernel, alpha=alpha, ng=ng, ep=ep),
        out_shape=jax.ShapeDtypeStruct((nb, s, d), F32),
        grid=(nb, s // tm, ne // ep),
        in_specs=[pl.BlockSpec((1, tm, d), tok),
                  pl.BlockSpec((1, tm, V7X_LANES), tok),
                  pl.BlockSpec((1, tm, d), tok),
                  pl.BlockSpec((1, mrows, d), mod),
                  pl.BlockSpec((ep, d, f), lambda b, i, e: (e, 0, 0)),
                  pl.BlockSpec((ep, d, f), lambda b, i, e: (e, 0, 0)),
                  pl.BlockSpec((ep * f, d), lambda b, i, e: (e, 0)),
                  pl.BlockSpec((1, d), const),
                  pl.BlockSpec((1, d), const)],
        out_specs=pl.BlockSpec((1, tm, d), tok),
        scratch_shapes=[pltpu.VMEM((tm, d), F32)],
        compiler_params=pltpu.CompilerParams(
            dimension_semantics=("arbitrary", "arbitrary", "arbitrary"),
            vmem_limit_bytes=V7X_VMEM_BUDGET),
        name="moe_experts",
    )(h2, gates, x1, g2, w_gate, w_up, w_down2, ln_g, ln_b)


def _pad_to(a, axis, size):
    if a.shape[axis] == size:
        return a
    pad = [(0, 0)] * a.ndim
    pad[axis] = (0, size - a.shape[axis])
    return jnp.pad(a, pad)


def kernel(x_prompt, x_sample, c_prompt, c_sample, cache_k, cache_v, cache_logf, page_table, w_in, b_forget, g_mix, w_out, w_ada, b_ada, ln1_g, ln1_b, ln2_g, ln2_b, w_router_group, b_router_group, w_router_expert, b_router_expert, w_gate_e, w_up_e, w_down_e):
    b, s, d = x_prompt.shape
    db, n_tok, _ = x_sample.shape
    depth = w_in.shape[0]
    n_pool, _, ps, nh, hd = cache_k.shape
    nf = b_forget.shape[1]
    mix = nh * hd
    ng = w_router_group.shape[2]
    ne = w_router_expert.shape[2]
    epg = ne // ng
    f = w_gate_e.shape[3]
    nfp = _round_up(nf, V7X_SUBLANES)
    alpha = float((2 * depth) ** 0.25)
    assert nf % 2 == 0 and (nh - nf) % 2 == 0 and 2 * hd == V7X_LANES
    assert ng + ne <= V7X_LANES and n_tok <= ps

    cache_kt = jnp.transpose(cache_k, (0, 1, 3, 4, 2)).reshape(n_pool, depth, mix, ps)
    cache_vt = jnp.transpose(cache_v, (0, 1, 3, 4, 2)).reshape(n_pool, depth, mix, ps)
    cache_lf_t = _pad_to(jnp.swapaxes(cache_logf, 2, 3), 2, nfp)

    n_c = b + db
    c_all = _pad_to(jnp.concatenate([c_prompt, c_sample], axis=0), 0, _round_up(n_c, V7X_SUBLANES))

    yp, ys = x_prompt, x_sample.reshape(1, db * n_tok, d)
    outs = {name: [] for name in ("kp", "vp", "lp", "ks", "vs", "ls")}
    for l in range(depth):
        mods = _adaln(c_all, w_ada[l], b_ada[l])
        mods_p = [m.reshape(b, 1, d) for m in jnp.split(mods[:b], 6, axis=-1)]
        mods_s = [jnp.repeat(m, n_tok, axis=0).reshape(1, db * n_tok, d)
                  for m in jnp.split(mods[b:n_c], 6, axis=-1)]

        w_qkv = w_in[l][:, :3 * mix].astype(BF16)
        w_f = w_in[l][:, 3 * mix:]
        wf = _pad_to(w_f, 1, V7X_LANES).astype(BF16)
        wft = _pad_to(w_f.T, 0, nfp).astype(BF16)
        bf = b_forget[l].reshape(1, nf)
        bft = _pad_to(b_forget[l].reshape(nf, 1), 0, nfp)
        g_mix2 = g_mix[l].reshape(1, mix)
        w_out_b = w_out[l].astype(BF16)
        w_r = _pad_to(jnp.concatenate([w_router_group[l], w_router_expert[l]], axis=1), 1, V7X_LANES)
        w_r_hi, w_r_lo = _split2(w_r)
        w_r = jnp.concatenate([w_r_hi, w_r_hi, w_r_lo], axis=0)
        b_r = _pad_to(jnp.concatenate([b_router_group[l], b_router_expert[l]]).reshape(1, ng + ne),
                      1, V7X_LANES)
        wg_b = w_gate_e[l].astype(BF16)
        wu_b = w_up_e[l].astype(BF16)
        wd_b = w_down_e[l].astype(BF16).reshape(ne * f, d)
        lg1, lb1 = ln1_g[l].reshape(1, d), ln1_b[l].reshape(1, d)
        lg2, lb2 = ln2_g[l].reshape(1, d), ln2_b[l].reshape(1, d)
        proj = functools.partial(_inproj, w_qkv=w_qkv, wf=wf, wft=wft, bf=bf, bft=bft,
                                 mix=mix, nf=nf, hd=hd)
        ffn = functools.partial(_moe, w_gate=wg_b, w_up=wu_b, w_down2=wd_b, ln_g=lg2, ln_b=lb2,
                                alpha=alpha, ng=ng)
        mixer_out = functools.partial(_outproj, w_out=w_out_b, ln_g=lg1, ln_b=lb1, w_r=w_r, b_r=b_r,
                                      alpha=alpha, ng=ng, epg=epg)

        sh1, sc1, g1, sh2, sc2, g2 = mods_p
        q, k, v, kb, vb, lf, cum = proj(yp, sc1, sh1)
        cum4 = cum[:, :nf].reshape(b, nf // 2, 2, s)
        mixn = _prompt_attention(q, kb, vb, cum4, g_mix2, nf=nf, hd=hd)
        x1, h2, gates = mixer_out(mixn, yp, g1, sc2, sh2)
        yp = ffn(h2, gates, x1, g2)
        outs["kp"].append(k.reshape(b, s, nh, hd))
        outs["vp"].append(v.reshape(b, s, nh, hd))
        outs["lp"].append(lf)

        sh1, sc1, g1, sh2, sc2, g2 = mods_s
        q, k, v, kb, vb, lf, _ = proj(ys, sc1, sh1)
        q_s = q.reshape(db, n_tok, mix).astype(F32)
        k_new_t = _pad_to(jnp.swapaxes(kb.reshape(db, n_tok, mix), 1, 2), 2, ps)
        v_new_t = _pad_to(jnp.swapaxes(vb.reshape(db, n_tok, mix), 1, 2), 2, ps)
        lf_new_t = _pad_to(_pad_to(jnp.swapaxes(lf.reshape(db, n_tok, nf), 1, 2), 1, nfp), 2, ps)
        mixn = _decode_attention(q_s, k_new_t, v_new_t, lf_new_t, cache_kt, cache_vt, cache_lf_t,
                                 page_table, g_mix2, layer=l, nh=nh, nf=nf, hd=hd)
        x1, h2, gates = mixer_out((mixn.reshape(1, db * n_tok, mix),), ys, g1, sc2, sh2)
        ys = ffn(h2, gates, x1, g2)
        outs["ks"].append(k.reshape(db, n_tok, nh, hd))
        outs["vs"].append(v.reshape(db, n_tok, nh, hd))
        outs["ls"].append(lf.reshape(db, n_tok, nf))

    stack = lambda name: jnp.stack(outs[name], axis=1)
    return (yp, ys.reshape(db, n_tok, d), stack("kp"), stack("vp"), stack("lp"),
            stack("ks"), stack("vs"), stack("ls"))
```

```python
import functools

import jax
import jax.numpy as jnp
from jax import lax
from jax.experimental import pallas as pl
from jax.experimental.pallas import tpu as pltpu

F32 = jnp.float32
BF16 = jnp.bfloat16
LN_EPS = 1e-5
RMS_EPS = 1e-6
LOG2E = 1.4426950408889634
SB_DEAD_MASS = 104.0
FOX_FOLD_PARTS = 3

V7X_LANES = 128
V7X_SUBLANES = 8
V7X_MXU_DIM = 256
V7X_VMEM_BUDGET = 56 * 1024 * 1024

_NT = (((1,), (1,)), ((), ()))


def _dot(a, b):
    return jnp.dot(a, b, preferred_element_type=F32)


def _dot_nt(a, b):
    return lax.dot_general(a, b, _NT, preferred_element_type=F32)


def _round_up(n, m):
    return (n + m - 1) // m * m


def _split3(x):
    hi = x.astype(BF16)
    r1 = x - hi.astype(F32)
    mid = r1.astype(BF16)
    lo = (r1 - mid.astype(F32)).astype(BF16)
    return hi, mid, lo


def _split2(x):
    hi = x.astype(BF16)
    lo = (x - hi.astype(F32)).astype(BF16)
    return hi, lo


def _dot_parts(parts, tri_stack):
    width = tri_stack.shape[1]
    return _dot(jnp.concatenate(parts, axis=-1), tri_stack[0:len(parts) * width])


def _stack_rows(tri, n):
    return jnp.concatenate([tri] * n, axis=0)


def _softplus_neg_abs(z):
    return jnp.log(1.0 + jnp.exp2(jnp.abs(z) * (-LOG2E)))


def _layer_norm(x, g, b):
    mu = jnp.mean(x, axis=-1, keepdims=True)
    xc = x - mu
    var = jnp.mean(xc * xc, axis=-1, keepdims=True)
    return xc * lax.rsqrt(var + LN_EPS) * g + b


def _adaln_kernel(c_ref, w_ref, b_ref, o_ref):
    c = c_ref[...]
    sc = c * (1.0 / (1.0 + jnp.exp(-c)))
    o_ref[...] = jnp.dot(sc, w_ref[...], precision=lax.Precision.HIGHEST,
                         preferred_element_type=F32) + b_ref[...]


def _adaln(c_all, w_ada, b_ada):
    rows, d = c_all.shape
    n = w_ada.shape[1]
    tn = min(n, 512)
    return pl.pallas_call(
        _adaln_kernel,
        out_shape=jax.ShapeDtypeStruct((rows, n), F32),
        grid=(n // tn,),
        in_specs=[pl.BlockSpec((rows, d), lambda j: (0, 0)),
                  pl.BlockSpec((d, tn), lambda j: (0, j)),
                  pl.BlockSpec((1, tn), lambda j: (0, j))],
        out_specs=pl.BlockSpec((rows, tn), lambda j: (0, j)),
        compiler_params=pltpu.CompilerParams(dimension_semantics=("arbitrary",)),
        name="adaln",
    )(c_all, w_ada, b_ada.reshape(1, n))


def _inproj_kernel(x_ref, sc_ref, sh_ref, w_ref, wf_ref, wft_ref, bf_ref, bft_ref, tri_ref, qs_ref,
                   q_ref, k_ref, v_ref, kb_ref, vb_ref, lf_ref, cum_ref, carry_ref,
                   *, mix, nf):
    si = pl.program_id(1)
    h = (x_ref[0] * (1.0 + sc_ref[0]) + sh_ref[0]).astype(BF16)
    q_ref[0] = (_dot(h, w_ref[:, 0:mix]) * qs_ref[...]).astype(BF16)
    k = _dot(h, w_ref[:, mix:2 * mix])
    k_ref[0] = k
    kb_ref[0] = k.astype(BF16)
    v = _dot(h, w_ref[:, 2 * mix:3 * mix])
    v_ref[0] = v
    vb_ref[0] = v.astype(BF16)

    logit = _dot(h, wf_ref[...])[:, 0:nf] + bf_ref[...]
    lf_ref[0] = jnp.minimum(logit, 0.0) - _softplus_neg_abs(logit)

    logit_t = _dot_nt(wft_ref[...], h) + bft_ref[...]
    lf_t = jnp.minimum(logit_t, 0.0) - _softplus_neg_abs(logit_t)
    local = _dot_parts(_split3(lf_t), tri_ref[...])

    @pl.when(si == 0)
    def _():
        carry_ref[...] = jnp.zeros_like(carry_ref)

    cum = local + carry_ref[:, 0:1]
    cum_ref[0] = cum
    ts = cum.shape[1]
    carry_ref[...] = jnp.broadcast_to(cum[:, ts - 1:ts], carry_ref.shape)


def _inproj(x, sc, sh, w_qkv, wf, wft, bf, bft, *, mix, nf, hd):
    nb, s, d = x.shape
    mrows = sc.shape[1]
    ts = min(s, 512)
    assert s % ts == 0
    nfp = wft.shape[0]
    tri = _stack_rows((jnp.arange(ts)[:, None] <= jnp.arange(ts)[None, :]).astype(BF16), 3)
    col_head = jnp.arange(mix) // hd
    q_scale = (jnp.where(col_head < nf, LOG2E, 1.0) * float(hd) ** -0.5).astype(F32).reshape(1, mix)
    tok = lambda b, i: (b, i, 0)
    mod = lambda b, i: (b, 0, 0)
    const = lambda b, i: (0, 0)
    return pl.pallas_call(
        functools.partial(_inproj_kernel, mix=mix, nf=nf),
        out_shape=(jax.ShapeDtypeStruct((nb, s, mix), BF16),
                   jax.ShapeDtypeStruct((nb, s, mix), F32),
                   jax.ShapeDtypeStruct((nb, s, mix), F32),
                   jax.ShapeDtypeStruct((nb, s, mix), BF16),
                   jax.ShapeDtypeStruct((nb, s, mix), BF16),
                   jax.ShapeDtypeStruct((nb, s, nf), F32),
                   jax.ShapeDtypeStruct((nb, nfp, s), F32)),
        grid=(nb, s // ts),
        in_specs=[pl.BlockSpec((1, ts, d), tok),
                  pl.BlockSpec((1, mrows, d), mod),
                  pl.BlockSpec((1, mrows, d), mod),
                  pl.BlockSpec((d, 3 * mix), const),
                  pl.BlockSpec((d, V7X_LANES), const),
                  pl.BlockSpec((nfp, d), const),
                  pl.BlockSpec((1, nf), const),
                  pl.BlockSpec((nfp, 1), const),
                  pl.BlockSpec((3 * ts, ts), const),
                  pl.BlockSpec((1, mix), const)],
        out_specs=(pl.BlockSpec((1, ts, mix), tok),
                   pl.BlockSpec((1, ts, mix), tok),
                   pl.BlockSpec((1, ts, mix), tok),
                   pl.BlockSpec((1, ts, mix), tok),
                   pl.BlockSpec((1, ts, mix), tok),
                   pl.BlockSpec((1, ts, nf), tok),
                   pl.BlockSpec((1, nfp, ts), lambda b, i: (b, 0, i))),
        scratch_shapes=[pltpu.VMEM((nfp, V7X_LANES), F32)],
        compiler_params=pltpu.CompilerParams(
            dimension_semantics=("arbitrary", "arbitrary"),
            vmem_limit_bytes=V7X_VMEM_BUDGET),
        name="inproj",
    )(x, sc, sh, w_qkv, wf, wft, bf, bft, tri, q_scale)


def _pair_rmsnorm(outs, lane, hd, g):
    first = (lane // hd) == 0
    o = jnp.where(first, outs[0], outs[1])
    osq = o * o
    ms0 = jnp.sum(jnp.where(first, osq, 0.0), axis=-1, keepdims=True)
    ms1 = jnp.sum(jnp.where(first, 0.0, osq), axis=-1, keepdims=True)
    ms = jnp.where(first, ms0, ms1) * (1.0 / hd)
    return o * lax.rsqrt(ms + RMS_EPS) * g


def _fox_kernel(qt_ref, k_ref, vt_ref, cum_ref, g_ref, o_ref, ka_ref, va_ref, *, tq, hd, chunk):
    qi = pl.program_id(2)
    s_len = k_ref.shape[1]
    lane = lax.broadcasted_iota(jnp.int32, (1, V7X_LANES), 1)
    feat = lax.broadcasted_iota(jnp.int32, (V7X_LANES, 1), 0)

    @pl.when(qi == 0)
    def _():
        for c in range(s_len // chunk):
            rows = slice(c * chunk, (c + 1) * chunk)
            k = k_ref[0, rows, :].astype(F32)
            vt = vt_ref[0, :, rows].astype(F32)
            for h in range(2):
                spare = (1 - h) * hd
                ka = jnp.where((lane // hd) == h, k, 0.0)
                for i, part in enumerate(_split3(cum_ref[0, 0, rows, h:h + 1] * LOG2E)):
                    ka = jnp.where(lane == spare + i, part.astype(F32), ka)
                ka_ref[h, rows, :] = ka.astype(BF16)
                va_ref[h, :, rows] = jnp.where((feat // hd) == h, vt, 1.0).astype(BF16)

    featq = lax.broadcasted_iota(jnp.int32, (V7X_LANES, tq), 0)
    qtf = qt_ref[0].astype(F32)
    qa = []
    for h in range(2):
        spare = (1 - h) * hd
        fold = (featq >= spare) & (featq < spare + FOX_FOLD_PARTS)
        qa.append(jnp.where((featq // hd) == h, qtf, jnp.where(fold, -1.0, 0.0)).astype(BF16))
    krow = lax.broadcasted_iota(jnp.int32, (tq, tq), 0)
    qcol = lax.broadcasted_iota(jnp.int32, (tq, tq), 1)

    def step(kb, carry, masked):
        k0 = pl.multiple_of(kb * tq, tq)
        new = []
        for h in range(2):
            m, acc = carry[h]
            s = _dot(ka_ref[h, pl.ds(k0, tq), :], qa[h])
            if masked:
                s = jnp.where(krow <= qcol, s, -jnp.inf)
            m_new = jnp.maximum(m, jnp.max(s, axis=0, keepdims=True))
            alpha = jnp.exp2(m - m_new)
            p = jnp.exp2(s - m_new).astype(BF16)
            new.append((m_new, alpha * acc + _dot(va_ref[h, :, pl.ds(k0, tq)], p)))
        return tuple(new)

    init = tuple((jnp.full((1, tq), -jnp.inf, F32), jnp.zeros((V7X_LANES, tq), F32)) for _ in range(2))
    carry = lax.fori_loop(0, qi, functools.partial(step, masked=False), init)
    carry = step(qi, carry, True)
    first = (feat // hd) == 0
    den0 = carry[0][1][hd:hd + 1, :]
    den1 = carry[1][1][0:1, :]
    o = jnp.where(first, carry[0][1] / den0, carry[1][1] / den1)
    osq = o * o
    ms0 = jnp.sum(jnp.where(first, osq, 0.0), axis=0, keepdims=True)
    ms1 = jnp.sum(jnp.where(first, 0.0, osq), axis=0, keepdims=True)
    ms = jnp.where(first, ms0, ms1) * (1.0 / hd)
    o_ref[0] = (o * lax.rsqrt(ms + RMS_EPS) * g_ref[...]).T.astype(BF16)


def _sb_kernel(q_ref, k_ref, v_ref, tri_ref, g_ref, o_ref, *, tq, sub, hd):
    qi = pl.program_id(2)
    q = q_ref[0]
    lane = lax.broadcasted_iota(jnp.int32, (1, V7X_LANES), 1)
    row = lax.broadcasted_iota(jnp.int32, (tq, tq), 0)
    col = lax.broadcasted_iota(jnp.int32, (tq, tq), 1)
    qh = [jnp.where((lane // hd) == h, q, jnp.zeros_like(q)) for h in range(2)]
    n_sub = tq // sub

    def step(kb, carry, qs, masked):
        k0 = pl.multiple_of(kb * tq, tq)
        k = k_ref[0, pl.ds(k0, tq), :]
        v = v_ref[0, pl.ds(k0, tq), :]
        tri = tri_ref[...]
        new = []
        for h in range(2):
            rsum, acc = carry[h]
            z = _dot_nt(qs[h], k)
            sp = jnp.maximum(z, 0.0) + _softplus_neg_abs(z)
            lsz = z - sp
            if masked:
                valid = col < row
                sp = jnp.where(valid, sp, 0.0)
            afters = []
            later = rsum
            for c in reversed(range(n_sub)):
                part = sp[:, c * sub:(c + 1) * sub]
                afters.append(_dot(part.astype(BF16), tri) + later)
                later = later + jnp.sum(part, axis=-1, keepdims=True)
            after = afters[0] if n_sub == 1 else jnp.concatenate(afters[::-1], axis=-1)
            a = jnp.exp(lsz - after)
            if masked:
                a = jnp.where(valid, a, 0.0)
            new.append((later, acc + _dot(a.astype(BF16), v)))
        return tuple(new)

    def least_mass(carry, first_row=0):
        return jnp.minimum(jnp.min(carry[0][0][first_row:]), jnp.min(carry[1][0][first_row:]))

    def walk_earlier(start, carry, qs, first_row=0):
        def more_to_do(state):
            i, _, least = state
            return jnp.logical_and(i < qi, least < SB_DEAD_MASS)

        def earlier_block(state):
            i, c, _ = state
            c = step(qi - 1 - i, c, qs, False)
            return i + 1, c, least_mass(c, first_row)

        i, carry, _ = lax.while_loop(more_to_do, earlier_block, (start, carry, least_mass(carry, first_row)))
        return i, carry

    init = tuple((jnp.zeros((tq, 1), F32), jnp.zeros((tq, V7X_LANES), F32)) for _ in range(2))
    carry = step(qi, init, qh, True)
    top = tq // 2
    i, carry = walk_earlier(jnp.int32(0), carry, qh, first_row=top)
    _, upper = walk_earlier(i, tuple((r[:top], a[:top]) for (r, a) in carry), [x[:top] for x in qh])
    carry = tuple((r, jnp.concatenate([ua, a[top:]], axis=0)) for (_, ua), (r, a) in zip(upper, carry))
    outs = [acc for (_, acc) in carry]
    o_ref[0] = _pair_rmsnorm(outs, lane, hd, g_ref[...]).astype(BF16)


def _prompt_attention(q, kb, vb, cum4, g_mix2, *, nf, hd):
    b, s, mix = q.shape
    nh = mix // hd
    fox_pairs = nf // 2
    sb_pairs = (nh - nf) // 2
    tq_fox = min(s, 1024)
    tq_sb = min(s, 512)
    sub = min(tq_sb, V7X_MXU_DIM)
    assert s % tq_fox == 0 and s % tq_sb == 0 and tq_sb % sub == 0
    tri = (jnp.arange(sub)[:, None] > jnp.arange(sub)[None, :]).astype(BF16)
    params = pltpu.CompilerParams(
        dimension_semantics=("arbitrary", "arbitrary", "arbitrary"),
        vmem_limit_bytes=V7X_VMEM_BUDGET)

    def specs(off, tq):
        return (pl.BlockSpec((1, tq, V7X_LANES), lambda bi, p, i: (bi, i, p + off)),
                pl.BlockSpec((1, s, V7X_LANES), lambda bi, p, i: (bi, 0, p + off)),
                pl.BlockSpec((1, s, V7X_LANES), lambda bi, p, i: (bi, 0, p + off)),
                pl.BlockSpec((1, V7X_LANES), lambda bi, p, i: (0, p + off)),
                pl.BlockSpec((1, tq, V7X_LANES), lambda bi, p, i: (bi, i, p)))

    fox_w = fox_pairs * V7X_LANES
    chunk = min(s, 512)
    assert s % chunk == 0
    q_t = jnp.swapaxes(q[:, :, :fox_w], 1, 2)
    v_t = jnp.swapaxes(vb[:, :, :fox_w], 1, 2)
    cum_col = jnp.swapaxes(cum4, 2, 3)
    o_fox = pl.pallas_call(
        functools.partial(_fox_kernel, tq=tq_fox, hd=hd, chunk=chunk),
        out_shape=jax.ShapeDtypeStruct((b, s, fox_w), BF16),
        grid=(b, fox_pairs, s // tq_fox),
        in_specs=[pl.BlockSpec((1, V7X_LANES, tq_fox), lambda bi, p, i: (bi, p, i)),
                  pl.BlockSpec((1, s, V7X_LANES), lambda bi, p, i: (bi, 0, p)),
                  pl.BlockSpec((1, V7X_LANES, s), lambda bi, p, i: (bi, p, 0)),
                  pl.BlockSpec((1, 1, s, 2), lambda bi, p, i: (bi, p, 0, 0)),
                  pl.BlockSpec((V7X_LANES, 1), lambda bi, p, i: (p, 0))],
        out_specs=pl.BlockSpec((1, tq_fox, V7X_LANES), lambda bi, p, i: (bi, i, p)),
        scratch_shapes=[pltpu.VMEM((2, s, V7X_LANES), BF16), pltpu.VMEM((2, V7X_LANES, s), BF16)],
        compiler_params=params,
        name="fox_prompt",
    )(q_t, kb, v_t, cum_col, g_mix2.reshape(mix, 1))
    qs, ks, vs, gs, os_ = specs(fox_pairs, tq_sb)
    o_sb = pl.pallas_call(
        functools.partial(_sb_kernel, tq=tq_sb, sub=sub, hd=hd),
        out_shape=jax.ShapeDtypeStruct((b, s, sb_pairs * V7X_LANES), BF16),
        grid=(b, sb_pairs, s // tq_sb),
        in_specs=[qs, ks, vs, pl.BlockSpec((sub, sub), lambda bi, p, i: (0, 0)), gs],
        out_specs=os_,
        compiler_params=params,
        name="sb_prompt",
    )(q, kb, vb, tri, g_mix2)
    return o_fox, o_sb


def _decode_kernel(pt_ref, q_ref, knt_ref, vnt_ref, lfn_ref, tri_ref, g_ref, *rest,
                   pps, n_tok, nh, nf, hd):
    del pt_ref
    k_refs = rest[0:pps]
    v_refs = rest[pps:2 * pps]
    lf_refs = rest[2 * pps:3 * pps]
    o_ref = rest[3 * pps]
    qbd_ref, m_ref, l_ref, r_ref, gc_ref, acc_ref = rest[3 * pps + 1:]
    j = pl.program_id(1)
    rows = n_tok * nh
    ps = tri_ref.shape[1]
    mix = nh * hd
    rowi = lax.broadcasted_iota(jnp.int32, (rows, 1), 0)
    row_h = rowi % nh
    row_t = rowi // nh
    is_fox = row_h < nf
    key = lax.broadcasted_iota(jnp.int32, (1, ps), 1)
    lane = lax.broadcasted_iota(jnp.int32, (1, mix), 1)

    def process(blocks, new):
        tri = tri_ref[...]
        qbd = qbd_ref[...]
        g_off = gc_ref[0:nf, 0:1]
        r_off = r_ref[:, 0:1]
        xs, a_sbs = [], []
        for k_t, _, lf_t in blocks:
            s = _dot(qbd, k_t)
            lf = lf_t[0:nf]
            g_suffix = (_dot_parts(_split3(lf), tri) + g_off) * LOG2E
            g_off = g_off + jnp.sum(lf, axis=-1, keepdims=True)
            if nh > nf:
                g_suffix = jnp.concatenate([g_suffix, jnp.zeros((nh - nf, ps), F32)], axis=0)
            x = s + jnp.concatenate([g_suffix] * n_tok, axis=0)
            if new:
                x = jnp.where((key < n_tok) & (key <= row_t), x, -jnp.inf)
            xs.append(x)
            sp = jnp.maximum(s, 0.0) + _softplus_neg_abs(s)
            lsz = s - sp
            if new:
                valid = key < row_t
                sp = jnp.where(valid, sp, 0.0)
            a_sb = jnp.exp(lsz - (_dot(sp.astype(BF16), tri[0:ps]) + r_off))
            if new:
                a_sb = jnp.where(valid, a_sb, 0.0)
            a_sbs.append(a_sb)
            r_off = r_off + jnp.sum(sp, axis=-1, keepdims=True)
        gc_ref[0:nf, :] = jnp.broadcast_to(g_off, (nf, gc_ref.shape[1]))
        r_ref[...] = jnp.broadcast_to(r_off, r_ref.shape)
        m_prev = m_ref[:, 0:1]
        m_new = m_prev
        for x in xs:
            m_new = jnp.maximum(m_new, jnp.max(x, axis=-1, keepdims=True))
        alpha = jnp.exp2(m_prev - m_new)
        l_new = alpha * l_ref[:, 0:1]
        pv = None
        for x, a_sb, (_, v_t, _) in zip(xs, a_sbs, blocks):
            p_fox = jnp.exp2(x - m_new)
            l_new = l_new + jnp.sum(p_fox, axis=-1, keepdims=True)
            d = _dot_nt(jnp.where(is_fox, p_fox, a_sb).astype(BF16), v_t)
            pv = d if pv is None else pv + d
        m_ref[...] = jnp.broadcast_to(m_new, m_ref.shape)
        l_ref[...] = jnp.broadcast_to(l_new, l_ref.shape)
        acc_ref[...] = acc_ref[...] * jnp.where(is_fox, alpha, 1.0) + pv

    @pl.when(j == 0)
    def _():
        q = q_ref[0]
        head_rows = lax.broadcasted_iota(jnp.int32, (nh, 1), 0)
        own = (lane // hd) == head_rows
        qbd_ref[...] = jnp.concatenate(
            [jnp.where(own, jnp.broadcast_to(q[t:t + 1], (nh, mix)), 0.0) for t in range(n_tok)],
            axis=0).astype(BF16)
        m_ref[...] = jnp.full(m_ref.shape, -jnp.inf, F32)
        l_ref[...] = jnp.zeros_like(l_ref)
        r_ref[...] = jnp.zeros_like(r_ref)
        gc_ref[...] = jnp.zeros_like(gc_ref)
        acc_ref[...] = jnp.zeros_like(acc_ref)
        process([(knt_ref[0], vnt_ref[0], lfn_ref[0])], True)

    process([(k_refs[i][0, 0].astype(BF16), v_refs[i][0, 0].astype(BF16), lf_refs[i][0, 0])
             for i in range(pps)], False)

    @pl.when(j == pl.num_programs(1) - 1)
    def _():
        o = acc_ref[...] * jnp.where(is_fox, 1.0 / l_ref[:, 0:1], 1.0)
        own = (lane // hd) == row_h
        ms = jnp.sum(jnp.where(own, o * o, 0.0), axis=-1, keepdims=True) * (1.0 / hd)
        on = jnp.where(own, o * lax.rsqrt(ms + RMS_EPS), 0.0)
        outs = [jnp.sum(on[t * nh:(t + 1) * nh], axis=0, keepdims=True) for t in range(n_tok)]
        o_ref[0] = (jnp.concatenate(outs, axis=0) * g_ref[...]).astype(BF16)


def _decode_attention(q, k_new_t, v_new_t, lf_new_t, cache_kt, cache_vt, cache_lf_t, page_table,
                      g_mix2, *, layer, nh, nf, hd):
    db, n_tok, mix = q.shape
    n_pages = page_table.shape[1]
    nfp, ps = cache_lf_t.shape[2:]
    pps = max(p for p in (16, 8, 4, 2, 1) if n_pages % p == 0)
    rows = n_tok * nh
    tri = _stack_rows((jnp.arange(ps)[:, None] > jnp.arange(ps)[None, :]).astype(BF16), 3)
    per_b = lambda b, j, pt: (b, 0, 0)
    const = lambda b, j, pt: (0, 0)

    def page_map(i):
        return lambda b, j, pt: (pt[b, n_pages - 1 - (j * pps + i)], layer, 0, 0)

    in_specs = [pl.BlockSpec((1, n_tok, mix), per_b),
                pl.BlockSpec((1, mix, ps), per_b),
                pl.BlockSpec((1, mix, ps), per_b),
                pl.BlockSpec((1, nfp, ps), per_b),
                pl.BlockSpec((3 * ps, ps), const),
                pl.BlockSpec((1, mix), const)]
    in_specs += [pl.BlockSpec((1, 1, mix, ps), page_map(i)) for i in range(pps)]
    in_specs += [pl.BlockSpec((1, 1, mix, ps), page_map(i)) for i in range(pps)]
    in_specs += [pl.BlockSpec((1, 1, nfp, ps), page_map(i)) for i in range(pps)]
    grid_spec = pltpu.PrefetchScalarGridSpec(
        num_scalar_prefetch=1,
        grid=(db, n_pages // pps),
        in_specs=in_specs,
        out_specs=pl.BlockSpec((1, n_tok, mix), per_b),
        scratch_shapes=[pltpu.VMEM((rows, mix), BF16),
                        pltpu.VMEM((rows, V7X_LANES), F32),
                        pltpu.VMEM((rows, V7X_LANES), F32),
                        pltpu.VMEM((rows, V7X_LANES), F32),
                        pltpu.VMEM((nfp, V7X_LANES), F32),
                        pltpu.VMEM((rows, mix), F32)])
    return pl.pallas_call(
        functools.partial(_decode_kernel, pps=pps, n_tok=n_tok, nh=nh, nf=nf, hd=hd),
        out_shape=jax.ShapeDtypeStruct((db, n_tok, mix), BF16),
        grid_spec=grid_spec,
        compiler_params=pltpu.CompilerParams(
            dimension_semantics=("arbitrary", "arbitrary"),
            vmem_limit_bytes=V7X_VMEM_BUDGET),
        name="decode_attn",
    )(page_table, q, k_new_t, v_new_t, lf_new_t, tri, g_mix2,
      *([cache_kt] * pps), *([cache_vt] * pps), *([cache_lf_t] * pps))


def _outproj_kernel(*refs, n_parts, alpha, ng, epg):
    mix_refs = refs[:n_parts]
    (x_ref, g1_ref, sc_ref, sh_ref, w_ref, lg_ref, lb_ref, wr_ref, br_ref,
     x1_ref, h2_ref, gate_ref) = refs[n_parts:]
    mo, off = None, 0
    for part in mix_refs:
        width = part.shape[2]
        term = _dot(part[0], w_ref[off:off + width, :])
        mo = term if mo is None else mo + term
        off += width
    x1 = _layer_norm(alpha * x_ref[0] + g1_ref[0] * mo, lg_ref[...], lb_ref[...])
    x1_ref[0] = x1
    h2 = x1 * (1.0 + sc_ref[0]) + sh_ref[0]
    h2_ref[0] = h2.astype(BF16)
    h_hi, h_lo = _split2(h2)
    logits = _dot(jnp.concatenate([h_hi, h_lo, h_hi], axis=-1), wr_ref[...]) + br_ref[...]
    lane = lax.broadcasted_iota(jnp.int32, logits.shape, 1)
    big = jnp.int32(2 ** 30)
    neg = -jnp.inf
    gl = jnp.where(lane < ng, logits, neg)
    gmax = jnp.max(gl, axis=-1, keepdims=True)
    gidx = jnp.min(jnp.where(gl == gmax, lane, big), axis=-1, keepdims=True)
    g_w = 1.0 / jnp.sum(jnp.exp(gl - gmax), axis=-1, keepdims=True)
    in_group = (lane >= ng + gidx * epg) & (lane < ng + (gidx + 1) * epg)
    el = jnp.where(in_group, logits, neg)
    m1 = jnp.max(el, axis=-1, keepdims=True)
    i1 = jnp.min(jnp.where(el == m1, lane, big), axis=-1, keepdims=True)
    el2 = jnp.where(lane == i1, neg, el)
    m2 = jnp.max(el2, axis=-1, keepdims=True)
    i2 = jnp.min(jnp.where(el2 == m2, lane, big), axis=-1, keepdims=True)
    den = jnp.sum(jnp.exp(el - m1), axis=-1, keepdims=True)
    p1 = 1.0 / den
    p2 = jnp.exp(m2 - m1) / den
    w1 = p1 / (p1 + p2)
    w2 = p2 / (p1 + p2)
    gate_ref[0] = jnp.where(lane == i1, g_w * w1, jnp.where(lane == i2, g_w * w2, 0.0))


def _outproj(mix_parts, x, g1, sc2, sh2, w_out, ln_g, ln_b, w_r, b_r, *, alpha, ng, epg):
    nb, s, d = x.shape
    mix = sum(part.shape[2] for part in mix_parts)
    mrows = g1.shape[1]
    ts = min(s, 512)
    assert s % ts == 0
    tok = lambda b, i: (b, i, 0)
    mod = lambda b, i: (b, 0, 0)
    const = lambda b, i: (0, 0)
    return pl.pallas_call(
        functools.partial(_outproj_kernel, n_parts=len(mix_parts), alpha=alpha, ng=ng, epg=epg),
        out_shape=(jax.ShapeDtypeStruct((nb, s, d), F32),
                   jax.ShapeDtypeStruct((nb, s, d), BF16),
                   jax.ShapeDtypeStruct((nb, s, V7X_LANES), F32)),
        grid=(nb, s // ts),
        in_specs=[pl.BlockSpec((1, ts, part.shape[2]), tok) for part in mix_parts] + [
                  pl.BlockSpec((1, ts, d), tok),
                  pl.BlockSpec((1, mrows, d), mod),
                  pl.BlockSpec((1, mrows, d), mod),
                  pl.BlockSpec((1, mrows, d), mod),
                  pl.BlockSpec((mix, d), const),
                  pl.BlockSpec((1, d), const),
                  pl.BlockSpec((1, d), const),
                  pl.BlockSpec((3 * d, V7X_LANES), const),
                  pl.BlockSpec((1, V7X_LANES), const)],
        out_specs=(pl.BlockSpec((1, ts, d), tok),
                   pl.BlockSpec((1, ts, d), tok),
                   pl.BlockSpec((1, ts, V7X_LANES), tok)),
        compiler_params=pltpu.CompilerParams(
            dimension_semantics=("arbitrary", "arbitrary"),
            vmem_limit_bytes=V7X_VMEM_BUDGET),
        name="outproj_router",
    )(*mix_parts, x, g1, sc2, sh2, w_out, ln_g, ln_b, w_r, b_r)


def _moe_kernel(h_ref, gate_ref, x1_ref, g2_ref, wg_ref, wu_ref, wd_ref, lg_ref, lb_ref,
                y_ref, acc_ref, *, alpha, ng, ep):
    e = pl.program_id(2)

    @pl.when(e == 0)
    def _():
        acc_ref[...] = jnp.zeros_like(acc_ref)

    h = h_ref[0]
    gates = gate_ref[0]
    lane = lax.broadcasted_iota(jnp.int32, gates.shape, 1)
    zs = []
    for i in range(ep):
        a = _dot(h, wg_ref[i])
        u = _dot(h, wu_ref[i])
        gcol = jnp.sum(jnp.where(lane == ng + e * ep + i, gates, 0.0), axis=-1, keepdims=True)
        zs.append((a * (1.0 / (1.0 + jnp.exp(-a))) * u * gcol).astype(BF16))
    zz = zs[0] if ep == 1 else jnp.concatenate(zs, axis=-1)
    acc_ref[...] += _dot(zz, wd_ref[...])

    @pl.when(e == pl.num_programs(2) - 1)
    def _():
        y_ref[0] = _layer_norm(alpha * x1_ref[0] + g2_ref[0] * acc_ref[...],
                               lg_ref[...], lb_ref[...])


def _moe(h2, gates, x1, g2, w_gate, w_up, w_down2, ln_g, ln_b, *, alpha, ng):
    nb, s, d = x1.shape
    ne, _, f = w_gate.shape
    mrows = g2.shape[1]
    tm = min(s, 1024)
    assert s % tm == 0
    ep = 4 if ne % 4 == 0 else 1
    tok = lambda b, i, e: (b, i, 0)
    mod = lambda b, i, e: (b, 0, 0)
    const = lambda b, i, e: (0, 0)
    return pl.pallas_call(
        functools.partial(_moe_kernel, alpha=alpha, ng=ng, ep=ep),
        out_shape=jax.ShapeDtypeStruct((nb, s, d), F32),
        grid=(nb, s // tm, ne // ep),
        in_specs=[pl.BlockSpec((1, tm, d), tok),
                  pl.BlockSpec((1, tm, V7X_LANES), tok),
                  pl.BlockSpec((1, tm, d), tok),
                  pl.BlockSpec((1, mrows, d), mod),
                  pl.BlockSpec((ep, d, f), lambda b, i, e: (e, 0, 0)),
                  pl.BlockSpec((ep, d, f), lambda b, i, e: (e, 0, 0)),
                  pl.BlockSpec((ep * f, d), lambda b, i, e: (e, 0)),
                  pl.BlockSpec((1, d), const),
                  pl.BlockSpec((1, d), const)],
        out_specs=pl.BlockSpec((1, tm, d), tok),
        scratch_shapes=[pltpu.VMEM((tm, d), F32)],
        compiler_params=pltpu.CompilerParams(
            dimension_semantics=("arbitrary", "arbitrary", "arbitrary"),
            vmem_limit_bytes=V7X_VMEM_BUDGET),
        name="moe_experts",
    )(h2, gates, x1, g2, w_gate, w_up, w_down2, ln_g, ln_b)


def _pad_to(a, axis, size):
    if a.shape[axis] == size:
        return a
    pad = [(0, 0)] * a.ndim
    pad[axis] = (0, size - a.shape[axis])
    return jnp.pad(a, pad)


def kernel(x_prompt, x_sample, c_prompt, c_sample, cache_k, cache_v, cache_logf, page_table, w_in, b_forget, g_mix, w_out, w_ada, b_ada, ln1_g, ln1_b, ln2_g, ln2_b, w_router_group, b_router_group, w_router_expert, b_router_expert, w_gate_e, w_up_e, w_down_e):
    b, s, d = x_prompt.shape
    db, n_tok, _ = x_sample.shape
    depth = w_in.shape[0]
    n_pool, _, ps, nh, hd = cache_k.shape
    nf = b_forget.shape[1]
    mix = nh * hd
    ng = w_router_group.shape[2]
    ne = w_router_expert.shape[2]
    epg = ne // ng
    f = w_gate_e.shape[3]
    nfp = _round_up(nf, V7X_SUBLANES)
    alpha = float((2 * depth) ** 0.25)
    assert nf % 2 == 0 and (nh - nf) % 2 == 0 and 2 * hd == V7X_LANES
    assert ng + ne <= V7X_LANES and n_tok <= ps

    cache_kt = jnp.transpose(cache_k, (0, 1, 3, 4, 2)).reshape(n_pool, depth, mix, ps)
    cache_vt = jnp.transpose(cache_v, (0, 1, 3, 4, 2)).reshape(n_pool, depth, mix, ps)
    cache_lf_t = _pad_to(jnp.swapaxes(cache_logf, 2, 3), 2, nfp)

    n_c = b + db
    c_all = _pad_to(jnp.concatenate([c_prompt, c_sample], axis=0), 0, _round_up(n_c, V7X_SUBLANES))

    yp, ys = x_prompt, x_sample.reshape(1, db * n_tok, d)
    outs = {name: [] for name in ("kp", "vp", "lp", "ks", "vs", "ls")}
    for l in range(depth):
        mods = _adaln(c_all, w_ada[l], b_ada[l])
        mods_p = [m.reshape(b, 1, d) for m in jnp.split(mods[:b], 6, axis=-1)]
        mods_s = [jnp.repeat(m, n_tok, axis=0).reshape(1, db * n_tok, d)
                  for m in jnp.split(mods[b:n_c], 6, axis=-1)]

        w_qkv = w_in[l][:, :3 * mix].astype(BF16)
        w_f = w_in[l][:, 3 * mix:]
        wf = _pad_to(w_f, 1, V7X_LANES).astype(BF16)
        wft = _pad_to(w_f.T, 0, nfp).astype(BF16)
        bf = b_forget[l].reshape(1, nf)
        bft = _pad_to(b_forget[l].reshape(nf, 1), 0, nfp)
        g_mix2 = g_mix[l].reshape(1, mix)
        w_out_b = w_out[l].astype(BF16)
        w_r = _pad_to(jnp.concatenate([w_router_group[l], w_router_expert[l]], axis=1), 1, V7X_LANES)
        w_r_hi, w_r_lo = _split2(w_r)
        w_r = jnp.concatenate([w_r_hi, w_r_hi, w_r_lo], axis=0)
        b_r = _pad_to(jnp.concatenate([b_router_group[l], b_router_expert[l]]).reshape(1, ng + ne),
                      1, V7X_LANES)
        wg_b = w_gate_e[l].astype(BF16)
        wu_b = w_up_e[l].astype(BF16)
        wd_b = w_down_e[l].astype(BF16).reshape(ne * f, d)
        lg1, lb1 = ln1_g[l].reshape(1, d), ln1_b[l].reshape(1, d)
        lg2, lb2 = ln2_g[l].reshape(1, d), ln2_b[l].reshape(1, d)
        proj = functools.partial(_inproj, w_qkv=w_qkv, wf=wf, wft=wft, bf=bf, bft=bft,
                                 mix=mix, nf=nf, hd=hd)
        ffn = functools.partial(_moe, w_gate=wg_b, w_up=wu_b, w_down2=wd_b, ln_g=lg2, ln_b=lb2,
                                alpha=alpha, ng=ng)
        mixer_out = functools.partial(_outproj, w_out=w_out_b, ln_g=lg1, ln_b=lb1, w_r=w_r, b_r=b_r,
                                      alpha=alpha, ng=ng, epg=epg)

        sh1, sc1, g1, sh2, sc2, g2 = mods_p
        q, k, v, kb, vb, lf, cum = proj(yp, sc1, sh1)
        cum4 = cum[:, :nf].reshape(b, nf // 2, 2, s)
        mixn = _prompt_attention(q, kb, vb, cum4, g_mix2, nf=nf, hd=hd)
        x1, h2, gates = mixer_out(mixn, yp, g1, sc2, sh2)
        yp = ffn(h2, gates, x1, g2)
        outs["kp"].append(k.reshape(b, s, nh, hd))
        outs["vp"].append(v.reshape(b, s, nh, hd))
        outs["lp"].append(lf)

        sh1, sc1, g1, sh2, sc2, g2 = mods_s
        q, k, v, kb, vb, lf, _ = proj(ys, sc1, sh1)
        q_s = q.reshape(db, n_tok, mix).astype(F32)
        k_new_t = _pad_to(jnp.swapaxes(kb.reshape(db, n_tok, mix), 1, 2), 2, ps)
        v_new_t = _pad_to(jnp.swapaxes(vb.reshape(db, n_tok, mix), 1, 2), 2, ps)
        lf_new_t = _pad_to(_pad_to(jnp.swapaxes(lf.reshape(db, n_tok, nf), 1, 2), 1, nfp), 2, ps)
        mixn = _decode_attention(q_s, k_new_t, v_new_t, lf_new_t, cache_kt, cache_vt, cache_lf_t,
                                 page_table, g_mix2, layer=l, nh=nh, nf=nf, hd=hd)
        x1, h2, gates = mixer_out((mixn.reshape(1, db * n_tok, mix),), ys, g1, sc2, sh2)
        ys = ffn(h2, gates, x1, g2)
        outs["ks"].append(k.reshape(db, n_tok, nh, hd))
        outs["vs"].append(v.reshape(db, n_tok, nh, hd))
        outs["ls"].append(lf.reshape(db, n_tok, nf))

    stack = lambda name: jnp.stack(outs[name], axis=1)
    return (yp, ys.reshape(db, n_tok, d), stack("kp"), stack("vp"), stack("lp"),
            stack("ks"), stack("vs"), stack("ls"))
```

```python
import functools

import jax
import jax.numpy as jnp
from jax import lax
from jax.experimental import pallas as pl
from jax.experimental.pallas import tpu as pltpu

F32 = jnp.float32
BF16 = jnp.bfloat16
LN_EPS = 1e-5
RMS_EPS = 1e-6
LOG2E = 1.4426950408889634
SB_DEAD_MASS = 104.0
FOX_FOLD_PARTS = 3

V7X_LANES = 128
V7X_SUBLANES = 8
V7X_MXU_DIM = 256
V7X_VMEM_BUDGET = 56 * 1024 * 1024

_NT = (((1,), (1,)), ((), ()))


def _dot(a, b):
    return jnp.dot(a, b, preferred_element_type=F32)


def _dot_nt(a, b):
    return lax.dot_general(a, b, _NT, preferred_element_type=F32)


def _round_up(n, m):
    return (n + m - 1) // m * m


def _split3(x):
    hi = x.astype(BF16)
    r1 = x - hi.astype(F32)
    mid = r1.astype(BF16)
    lo = (r1 - mid.astype(F32)).astype(BF16)
    return hi, mid, lo


def _split2(x):
    hi = x.astype(BF16)
    lo = (x - hi.astype(F32)).astype(BF16)
    return hi, lo


def _dot_parts(parts, tri_stack):
    width = tri_stack.shape[1]
    return _dot(jnp.concatenate(parts, axis=-1), tri_stack[0:len(parts) * width])


def _stack_rows(tri, n):
    return jnp.concatenate([tri] * n, axis=0)


def _softplus_neg_abs(z):
    return jnp.log(1.0 + jnp.exp2(jnp.abs(z) * (-LOG2E)))


def _layer_norm(x, g, b):
    mu = jnp.mean(x, axis=-1, keepdims=True)
    xc = x - mu
    var = jnp.mean(xc * xc, axis=-1, keepdims=True)
    return xc * lax.rsqrt(var + LN_EPS) * g + b


def _adaln_kernel(c_ref, w_ref, b_ref, o_ref):
    c = c_ref[...]
    sc = c * (1.0 / (1.0 + jnp.exp(-c)))
    o_ref[...] = jnp.dot(sc, w_ref[...], precision=lax.Precision.HIGHEST,
                         preferred_element_type=F32) + b_ref[...]


def _adaln(c_all, w_ada, b_ada):
    rows, d = c_all.shape
    n = w_ada.shape[1]
    tn = min(n, 512)
    return pl.pallas_call(
        _adaln_kernel,
        out_shape=jax.ShapeDtypeStruct((rows, n), F32),
        grid=(n // tn,),
        in_specs=[pl.BlockSpec((rows, d), lambda j: (0, 0)),
                  pl.BlockSpec((d, tn), lambda j: (0, j)),
                  pl.BlockSpec((1, tn), lambda j: (0, j))],
        out_specs=pl.BlockSpec((rows, tn), lambda j: (0, j)),
        compiler_params=pltpu.CompilerParams(dimension_semantics=("arbitrary",)),
        name="adaln",
    )(c_all, w_ada, b_ada.reshape(1, n))


def _inproj_kernel(x_ref, sc_ref, sh_ref, w_ref, wf_ref, wft_ref, bf_ref, bft_ref, tri_ref, qs_ref,
                   q_ref, k_ref, v_ref, kb_ref, vb_ref, lf_ref, cum_ref, qt_ref, vt_ref, carry_ref,
                   *, mix, nf):
    si = pl.program_id(1)
    fox_w = qt_ref.shape[1]
    h = (x_ref[0] * (1.0 + sc_ref[0]) + sh_ref[0]).astype(BF16)
    q = _dot(h, w_ref[:, 0:mix]) * qs_ref[...]
    q_ref[0] = q.astype(BF16)
    qt_ref[0] = q[:, 0:fox_w].T.astype(BF16)
    k = _dot(h, w_ref[:, mix:2 * mix])
    k_ref[0] = k
    kb_ref[0] = k.astype(BF16)
    v = _dot(h, w_ref[:, 2 * mix:3 * mix])
    v_ref[0] = v
    vb_ref[0] = v.astype(BF16)
    vt_ref[0] = v[:, 0:fox_w].T.astype(BF16)

    logit = _dot(h, wf_ref[...])[:, 0:nf] + bf_ref[...]
    lf_ref[0] = jnp.minimum(logit, 0.0) - _softplus_neg_abs(logit)

    logit_t = _dot_nt(wft_ref[...], h) + bft_ref[...]
    lf_t = jnp.minimum(logit_t, 0.0) - _softplus_neg_abs(logit_t)
    local = _dot_parts(_split3(lf_t), tri_ref[...])

    @pl.when(si == 0)
    def _():
        carry_ref[...] = jnp.zeros_like(carry_ref)

    cum = local + carry_ref[:, 0:1]
    cum_ref[0] = cum
    ts = cum.shape[1]
    carry_ref[...] = jnp.broadcast_to(cum[:, ts - 1:ts], carry_ref.shape)


def _inproj(x, sc, sh, w_qkv, wf, wft, bf, bft, *, mix, nf, hd):
    nb, s, d = x.shape
    mrows = sc.shape[1]
    ts = min(s, 512)
    assert s % ts == 0
    nfp = wft.shape[0]
    tri = _stack_rows((jnp.arange(ts)[:, None] <= jnp.arange(ts)[None, :]).astype(BF16), 3)
    col_head = jnp.arange(mix) // hd
    q_scale = (jnp.where(col_head < nf, LOG2E, 1.0) * float(hd) ** -0.5).astype(F32).reshape(1, mix)
    tok = lambda b, i: (b, i, 0)
    mod = lambda b, i: (b, 0, 0)
    const = lambda b, i: (0, 0)
    return pl.pallas_call(
        functools.partial(_inproj_kernel, mix=mix, nf=nf),
        out_shape=(jax.ShapeDtypeStruct((nb, s, mix), BF16),
                   jax.ShapeDtypeStruct((nb, s, mix), F32),
                   jax.ShapeDtypeStruct((nb, s, mix), F32),
                   jax.ShapeDtypeStruct((nb, s, mix), BF16),
                   jax.ShapeDtypeStruct((nb, s, mix), BF16),
                   jax.ShapeDtypeStruct((nb, s, nf), F32),
                   jax.ShapeDtypeStruct((nb, nfp, s), F32),
                   jax.ShapeDtypeStruct((nb, nf * hd, s), BF16),
                   jax.ShapeDtypeStruct((nb, nf * hd, s), BF16)),
        grid=(nb, s // ts),
        in_specs=[pl.BlockSpec((1, ts, d), tok),
                  pl.BlockSpec((1, mrows, d), mod),
                  pl.BlockSpec((1, mrows, d), mod),
                  pl.BlockSpec((d, 3 * mix), const),
                  pl.BlockSpec((d, V7X_LANES), const),
                  pl.BlockSpec((nfp, d), const),
                  pl.BlockSpec((1, nf), const),
                  pl.BlockSpec((nfp, 1), const),
                  pl.BlockSpec((3 * ts, ts), const),
                  pl.BlockSpec((1, mix), const)],
        out_specs=(pl.BlockSpec((1, ts, mix), tok),
                   pl.BlockSpec((1, ts, mix), tok),
                   pl.BlockSpec((1, ts, mix), tok),
                   pl.BlockSpec((1, ts, mix), tok),
                   pl.BlockSpec((1, ts, mix), tok),
                   pl.BlockSpec((1, ts, nf), tok),
                   pl.BlockSpec((1, nfp, ts), lambda b, i: (b, 0, i)),
                   pl.BlockSpec((1, nf * hd, ts), lambda b, i: (b, 0, i)),
                   pl.BlockSpec((1, nf * hd, ts), lambda b, i: (b, 0, i))),
        scratch_shapes=[pltpu.VMEM((nfp, V7X_LANES), F32)],
        compiler_params=pltpu.CompilerParams(
            dimension_semantics=("arbitrary", "arbitrary"),
            vmem_limit_bytes=V7X_VMEM_BUDGET),
        name="inproj",
    )(x, sc, sh, w_qkv, wf, wft, bf, bft, tri, q_scale)


def _pair_rmsnorm(outs, lane, hd, g):
    first = (lane // hd) == 0
    o = jnp.where(first, outs[0], outs[1])
    osq = o * o
    ms0 = jnp.sum(jnp.where(first, osq, 0.0), axis=-1, keepdims=True)
    ms1 = jnp.sum(jnp.where(first, 0.0, osq), axis=-1, keepdims=True)
    ms = jnp.where(first, ms0, ms1) * (1.0 / hd)
    return o * lax.rsqrt(ms + RMS_EPS) * g


def _fox_kernel(qt_ref, k_ref, vt_ref, cum_ref, g_ref, o_ref, ka_ref, va_ref, *, tq, hd, chunk):
    qi = pl.program_id(2)
    s_len = k_ref.shape[1]
    lane = lax.broadcasted_iota(jnp.int32, (1, V7X_LANES), 1)
    feat = lax.broadcasted_iota(jnp.int32, (V7X_LANES, 1), 0)

    @pl.when(qi == 0)
    def _():
        for c in range(s_len // chunk):
            rows = slice(c * chunk, (c + 1) * chunk)
            k = k_ref[0, rows, :].astype(F32)
            vt = vt_ref[0, :, rows].astype(F32)
            for h in range(2):
                spare = (1 - h) * hd
                ka = jnp.where((lane // hd) == h, k, 0.0)
                for i, part in enumerate(_split3(cum_ref[0, 0, rows, h:h + 1] * LOG2E)):
                    ka = jnp.where(lane == spare + i, part.astype(F32), ka)
                ka_ref[h, rows, :] = ka.astype(BF16)
                va_ref[h, :, rows] = jnp.where((feat // hd) == h, vt, 1.0).astype(BF16)

    featq = lax.broadcasted_iota(jnp.int32, (V7X_LANES, tq), 0)
    qtf = qt_ref[0].astype(F32)
    qa = []
    for h in range(2):
        spare = (1 - h) * hd
        fold = (featq >= spare) & (featq < spare + FOX_FOLD_PARTS)
        qa.append(jnp.where((featq // hd) == h, qtf, jnp.where(fold, -1.0, 0.0)).astype(BF16))
    krow = lax.broadcasted_iota(jnp.int32, (tq, tq), 0)
    qcol = lax.broadcasted_iota(jnp.int32, (tq, tq), 1)

    def step(kb, carry, masked):
        k0 = pl.multiple_of(kb * tq, tq)
        new = []
        for h in range(2):
            m, acc = carry[h]
            s = _dot(ka_ref[h, pl.ds(k0, tq), :], qa[h])
            if masked:
                s = jnp.where(krow <= qcol, s, -jnp.inf)
            m_new = jnp.maximum(m, jnp.max(s, axis=0, keepdims=True))
            alpha = jnp.exp2(m - m_new)
            p = jnp.exp2(s - m_new).astype(BF16)
            new.append((m_new, alpha * acc + _dot(va_ref[h, :, pl.ds(k0, tq)], p)))
        return tuple(new)

    init = tuple((jnp.full((1, tq), -jnp.inf, F32), jnp.zeros((V7X_LANES, tq), F32)) for _ in range(2))
    carry = lax.fori_loop(0, qi, functools.partial(step, masked=False), init)
    carry = step(qi, carry, True)
    first = (feat // hd) == 0
    den0 = carry[0][1][hd:hd + 1, :]
    den1 = carry[1][1][0:1, :]
    o = jnp.where(first, carry[0][1] / den0, carry[1][1] / den1)
    osq = o * o
    ms0 = jnp.sum(jnp.where(first, osq, 0.0), axis=0, keepdims=True)
    ms1 = jnp.sum(jnp.where(first, 0.0, osq), axis=0, keepdims=True)
    ms = jnp.where(first, ms0, ms1) * (1.0 / hd)
    o_ref[0] = (o * lax.rsqrt(ms + RMS_EPS) * g_ref[...]).T.astype(BF16)


def _sb_kernel(q_ref, k_ref, v_ref, tri_ref, g_ref, o_ref, *, tq, sub, hd):
    qi = pl.program_id(2)
    q = q_ref[0]
    lane = lax.broadcasted_iota(jnp.int32, (1, V7X_LANES), 1)
    row = lax.broadcasted_iota(jnp.int32, (tq, tq), 0)
    col = lax.broadcasted_iota(jnp.int32, (tq, tq), 1)
    qh = [jnp.where((lane // hd) == h, q, jnp.zeros_like(q)) for h in range(2)]
    n_sub = tq // sub

    def step(kb, carry, qs, masked):
        k0 = pl.multiple_of(kb * tq, tq)
        k = k_ref[0, pl.ds(k0, tq), :]
        v = v_ref[0, pl.ds(k0, tq), :]
        tri = tri_ref[...]
        new = []
        for h in range(2):
            rsum, acc = carry[h]
            z = _dot_nt(qs[h], k)
            sp = jnp.maximum(z, 0.0) + _softplus_neg_abs(z)
            lsz = z - sp
            if masked:
                valid = col < row
                sp = jnp.where(valid, sp, 0.0)
            afters = []
            later = rsum
            for c in reversed(range(n_sub)):
                part = sp[:, c * sub:(c + 1) * sub]
                afters.append(_dot(part.astype(BF16), tri) + later)
                later = later + jnp.sum(part, axis=-1, keepdims=True)
            after = afters[0] if n_sub == 1 else jnp.concatenate(afters[::-1], axis=-1)
            a = jnp.exp(lsz - after)
            if masked:
                a = jnp.where(valid, a, 0.0)
            new.append((later, acc + _dot(a.astype(BF16), v)))
        return tuple(new)

    def least_mass(carry, first_row=0):
        return jnp.minimum(jnp.min(carry[0][0][first_row:]), jnp.min(carry[1][0][first_row:]))

    def walk_earlier(start, carry, qs, first_row=0):
        def more_to_do(state):
            i, _, least = state
            return jnp.logical_and(i < qi, least < SB_DEAD_MASS)

        def earlier_block(state):
            i, c, _ = state
            c = step(qi - 1 - i, c, qs, False)
            return i + 1, c, least_mass(c, first_row)

        i, carry, _ = lax.while_loop(more_to_do, earlier_block, (start, carry, least_mass(carry, first_row)))
        return i, carry

    init = tuple((jnp.zeros((tq, 1), F32), jnp.zeros((tq, V7X_LANES), F32)) for _ in range(2))
    carry = step(qi, init, qh, True)
    top = tq // 2
    i, carry = walk_earlier(jnp.int32(0), carry, qh, first_row=top)
    _, upper = walk_earlier(i, tuple((r[:top], a[:top]) for (r, a) in carry), [x[:top] for x in qh])
    carry = tuple((r, jnp.concatenate([ua, a[top:]], axis=0)) for (_, ua), (r, a) in zip(upper, carry))
    outs = [acc for (_, acc) in carry]
    o_ref[0] = _pair_rmsnorm(outs, lane, hd, g_ref[...]).astype(BF16)


def _prompt_attention(q, kb, vb, q_t, v_t, cum4, g_mix2, *, nf, hd):
    b, s, mix = q.shape
    nh = mix // hd
    fox_pairs = nf // 2
    sb_pairs = (nh - nf) // 2
    tq_fox = min(s, 1024)
    tq_sb = min(s, 512)
    sub = min(tq_sb, V7X_MXU_DIM)
    assert s % tq_fox == 0 and s % tq_sb == 0 and tq_sb % sub == 0
    tri = (jnp.arange(sub)[:, None] > jnp.arange(sub)[None, :]).astype(BF16)
    params = pltpu.CompilerParams(
        dimension_semantics=("arbitrary", "arbitrary", "arbitrary"),
        vmem_limit_bytes=V7X_VMEM_BUDGET)

    def specs(off, tq):
        return (pl.BlockSpec((1, tq, V7X_LANES), lambda bi, p, i: (bi, i, p + off)),
                pl.BlockSpec((1, s, V7X_LANES), lambda bi, p, i: (bi, 0, p + off)),
                pl.BlockSpec((1, s, V7X_LANES), lambda bi, p, i: (bi, 0, p + off)),
                pl.BlockSpec((1, V7X_LANES), lambda bi, p, i: (0, p + off)),
                pl.BlockSpec((1, tq, V7X_LANES), lambda bi, p, i: (bi, i, p)))

    fox_w = fox_pairs * V7X_LANES
    chunk = min(s, 512)
    assert s % chunk == 0
    cum_col = jnp.swapaxes(cum4, 2, 3)
    o_fox = pl.pallas_call(
        functools.partial(_fox_kernel, tq=tq_fox, hd=hd, chunk=chunk),
        out_shape=jax.ShapeDtypeStruct((b, s, fox_w), BF16),
        grid=(b, fox_pairs, s // tq_fox),
        in_specs=[pl.BlockSpec((1, V7X_LANES, tq_fox), lambda bi, p, i: (bi, p, i)),
                  pl.BlockSpec((1, s, V7X_LANES), lambda bi, p, i: (bi, 0, p)),
                  pl.BlockSpec((1, V7X_LANES, s), lambda bi, p, i: (bi, p, 0)),
                  pl.BlockSpec((1, 1, s, 2), lambda bi, p, i: (bi, p, 0, 0)),
                  pl.BlockSpec((V7X_LANES, 1), lambda bi, p, i: (p, 0))],
        out_specs=pl.BlockSpec((1, tq_fox, V7X_LANES), lambda bi, p, i: (bi, i, p)),
        scratch_shapes=[pltpu.VMEM((2, s, V7X_LANES), BF16), pltpu.VMEM((2, V7X_LANES, s), BF16)],
        compiler_params=params,
        name="fox_prompt",
    )(q_t, kb, v_t, cum_col, g_mix2.reshape(mix, 1))
    qs, ks, vs, gs, os_ = specs(fox_pairs, tq_sb)
    o_sb = pl.pallas_call(
        functools.partial(_sb_kernel, tq=tq_sb, sub=sub, hd=hd),
        out_shape=jax.ShapeDtypeStruct((b, s, sb_pairs * V7X_LANES), BF16),
        grid=(b, sb_pairs, s // tq_sb),
        in_specs=[qs, ks, vs, pl.BlockSpec((sub, sub), lambda bi, p, i: (0, 0)), gs],
        out_specs=os_,
        compiler_params=params,
        name="sb_prompt",
    )(q, kb, vb, tri, g_mix2)
    return o_fox, o_sb


def _decode_kernel(pt_ref, q_ref, knt_ref, vnt_ref, lfn_ref, tri_ref, g_ref, *rest,
                   pps, n_tok, nh, nf, hd):
    del pt_ref
    k_refs = rest[0:pps]
    v_refs = rest[pps:2 * pps]
    lf_refs = rest[2 * pps:3 * pps]
    o_ref = rest[3 * pps]
    qbd_ref, m_ref, l_ref, r_ref, gc_ref, acc_ref = rest[3 * pps + 1:]
    j = pl.program_id(1)
    rows = n_tok * nh
    ps = tri_ref.shape[1]
    mix = nh * hd
    rowi = lax.broadcasted_iota(jnp.int32, (rows, 1), 0)
    row_h = rowi % nh
    row_t = rowi // nh
    is_fox = row_h < nf
    key = lax.broadcasted_iota(jnp.int32, (1, ps), 1)
    lane = lax.broadcasted_iota(jnp.int32, (1, mix), 1)

    def process(blocks, new):
        tri = tri_ref[...]
        qbd = qbd_ref[...]
        g_off = gc_ref[0:nf, 0:1]
        r_off = r_ref[:, 0:1]
        xs, a_sbs = [], []
        for k_t, _, lf_t in blocks:
            s = _dot(qbd, k_t)
            lf = lf_t[0:nf]
            g_suffix = (_dot_parts(_split3(lf), tri) + g_off) * LOG2E
            g_off = g_off + jnp.sum(lf, axis=-1, keepdims=True)
            if nh > nf:
                g_suffix = jnp.concatenate([g_suffix, jnp.zeros((nh - nf, ps), F32)], axis=0)
            x = s + jnp.concatenate([g_suffix] * n_tok, axis=0)
            if new:
                x = jnp.where((key < n_tok) & (key <= row_t), x, -jnp.inf)
            xs.append(x)
            sp = jnp.maximum(s, 0.0) + _softplus_neg_abs(s)
            lsz = s - sp
            if new:
                valid = key < row_t
                sp = jnp.where(valid, sp, 0.0)
            a_sb = jnp.exp(lsz - (_dot(sp.astype(BF16), tri[0:ps]) + r_off))
            if new:
                a_sb = jnp.where(valid, a_sb, 0.0)
            a_sbs.append(a_sb)
            r_off = r_off + jnp.sum(sp, axis=-1, keepdims=True)
        gc_ref[0:nf, :] = jnp.broadcast_to(g_off, (nf, gc_ref.shape[1]))
        r_ref[...] = jnp.broadcast_to(r_off, r_ref.shape)
        m_prev = m_ref[:, 0:1]
        m_new = m_prev
        for x in xs:
            m_new = jnp.maximum(m_new, jnp.max(x, axis=-1, keepdims=True))
        alpha = jnp.exp2(m_prev - m_new)
        l_new = alpha * l_ref[:, 0:1]
        pv = None
        for x, a_sb, (_, v_t, _) in zip(xs, a_sbs, blocks):
            p_fox = jnp.exp2(x - m_new)
            l_new = l_new + jnp.sum(p_fox, axis=-1, keepdims=True)
            d = _dot_nt(jnp.where(is_fox, p_fox, a_sb).astype(BF16), v_t)
            pv = d if pv is None else pv + d
        m_ref[...] = jnp.broadcast_to(m_new, m_ref.shape)
        l_ref[...] = jnp.broadcast_to(l_new, l_ref.shape)
        acc_ref[...] = acc_ref[...] * jnp.where(is_fox, alpha, 1.0) + pv

    @pl.when(j == 0)
    def _():
        q = q_ref[0]
        head_rows = lax.broadcasted_iota(jnp.int32, (nh, 1), 0)
        own = (lane // hd) == head_rows
        qbd_ref[...] = jnp.concatenate(
            [jnp.where(own, jnp.broadcast_to(q[t:t + 1], (nh, mix)), 0.0) for t in range(n_tok)],
            axis=0).astype(BF16)
        m_ref[...] = jnp.full(m_ref.shape, -jnp.inf, F32)
        l_ref[...] = jnp.zeros_like(l_ref)
        r_ref[...] = jnp.zeros_like(r_ref)
        gc_ref[...] = jnp.zeros_like(gc_ref)
        acc_ref[...] = jnp.zeros_like(acc_ref)
        process([(knt_ref[0], vnt_ref[0], lfn_ref[0])], True)

    process([(k_refs[i][0, 0].astype(BF16), v_refs[i][0, 0].astype(BF16), lf_refs[i][0, 0])
             for i in range(pps)], False)

    @pl.when(j == pl.num_programs(1) - 1)
    def _():
        o = acc_ref[...] * jnp.where(is_fox, 1.0 / l_ref[:, 0:1], 1.0)
        own = (lane // hd) == row_h
        ms = jnp.sum(jnp.where(own, o * o, 0.0), axis=-1, keepdims=True) * (1.0 / hd)
        on = jnp.where(own, o * lax.rsqrt(ms + RMS_EPS), 0.0)
        outs = [jnp.sum(on[t * nh:(t + 1) * nh], axis=0, keepdims=True) for t in range(n_tok)]
        o_ref[0] = (jnp.concatenate(outs, axis=0) * g_ref[...]).astype(BF16)


def _decode_attention(q, k_new_t, v_new_t, lf_new_t, cache_kt, cache_vt, cache_lf_t, page_table,
                      g_mix2, *, layer, nh, nf, hd):
    db, n_tok, mix = q.shape
    n_pages = page_table.shape[1]
    nfp, ps = cache_lf_t.shape[2:]
    pps = max(p for p in (16, 8, 4, 2, 1) if n_pages % p == 0)
    rows = n_tok * nh
    tri = _stack_rows((jnp.arange(ps)[:, None] > jnp.arange(ps)[None, :]).astype(BF16), 3)
    per_b = lambda b, j, pt: (b, 0, 0)
    const = lambda b, j, pt: (0, 0)

    def page_map(i):
        return lambda b, j, pt: (pt[b, n_pages - 1 - (j * pps + i)], layer, 0, 0)

    in_specs = [pl.BlockSpec((1, n_tok, mix), per_b),
                pl.BlockSpec((1, mix, ps), per_b),
                pl.BlockSpec((1, mix, ps), per_b),
                pl.BlockSpec((1, nfp, ps), per_b),
                pl.BlockSpec((3 * ps, ps), const),
                pl.BlockSpec((1, mix), const)]
    in_specs += [pl.BlockSpec((1, 1, mix, ps), page_map(i)) for i in range(pps)]
    in_specs += [pl.BlockSpec((1, 1, mix, ps), page_map(i)) for i in range(pps)]
    in_specs += [pl.BlockSpec((1, 1, nfp, ps), page_map(i)) for i in range(pps)]
    grid_spec = pltpu.PrefetchScalarGridSpec(
        num_scalar_prefetch=1,
        grid=(db, n_pages // pps),
        in_specs=in_specs,
        out_specs=pl.BlockSpec((1, n_tok, mix), per_b),
        scratch_shapes=[pltpu.VMEM((rows, mix), BF16),
                        pltpu.VMEM((rows, V7X_LANES), F32),
                        pltpu.VMEM((rows, V7X_LANES), F32),
                        pltpu.VMEM((rows, V7X_LANES), F32),
                        pltpu.VMEM((nfp, V7X_LANES), F32),
                        pltpu.VMEM((rows, mix), F32)])
    return pl.pallas_call(
        functools.partial(_decode_kernel, pps=pps, n_tok=n_tok, nh=nh, nf=nf, hd=hd),
        out_shape=jax.ShapeDtypeStruct((db, n_tok, mix), BF16),
        grid_spec=grid_spec,
        compiler_params=pltpu.CompilerParams(
            dimension_semantics=("arbitrary", "arbitrary"),
            vmem_limit_bytes=V7X_VMEM_BUDGET),
        name="decode_attn",
    )(page_table, q, k_new_t, v_new_t, lf_new_t, tri, g_mix2,
      *([cache_kt] * pps), *([cache_vt] * pps), *([cache_lf_t] * pps))


def _outproj_kernel(*refs, n_parts, alpha, ng, epg):
    mix_refs = refs[:n_parts]
    (x_ref, g1_ref, sc_ref, sh_ref, w_ref, lg_ref, lb_ref, wr_ref, br_ref,
     x1_ref, h2_ref, gate_ref) = refs[n_parts:]
    mo, off = None, 0
    for part in mix_refs:
        width = part.shape[2]
        term = _dot(part[0], w_ref[off:off + width, :])
        mo = term if mo is None else mo + term
        off += width
    x1 = _layer_norm(alpha * x_ref[0] + g1_ref[0] * mo, lg_ref[...], lb_ref[...])
    x1_ref[0] = x1
    h2 = x1 * (1.0 + sc_ref[0]) + sh_ref[0]
    h2_ref[0] = h2.astype(BF16)
    h_hi, h_lo = _split2(h2)
    logits = _dot(jnp.concatenate([h_hi, h_lo, h_hi], axis=-1), wr_ref[...]) + br_ref[...]
    lane = lax.broadcasted_iota(jnp.int32, logits.shape, 1)
    big = jnp.int32(2 ** 30)
    neg = -jnp.inf
    gl = jnp.where(lane < ng, logits, neg)
    gmax = jnp.max(gl, axis=-1, keepdims=True)
    gidx = jnp.min(jnp.where(gl == gmax, lane, big), axis=-1, keepdims=True)
    g_w = 1.0 / jnp.sum(jnp.exp(gl - gmax), axis=-1, keepdims=True)
    in_group = (lane >= ng + gidx * epg) & (lane < ng + (gidx + 1) * epg)
    el = jnp.where(in_group, logits, neg)
    m1 = jnp.max(el, axis=-1, keepdims=True)
    i1 = jnp.min(jnp.where(el == m1, lane, big), axis=-1, keepdims=True)
    el2 = jnp.where(lane == i1, neg, el)
    m2 = jnp.max(el2, axis=-1, keepdims=True)
    i2 = jnp.min(jnp.where(el2 == m2, lane, big), axis=-1, keepdims=True)
    den = jnp.sum(jnp.exp(el - m1), axis=-1, keepdims=True)
    p1 = 1.0 / den
    p2 = jnp.exp(m2 - m1) / den
    w1 = p1 / (p1 + p2)
    w2 = p2 / (p1 + p2)
    gate_ref[0] = jnp.where(lane == i1, g_w * w1, jnp.where(lane == i2, g_w * w2, 0.0))


def _outproj(mix_parts, x, g1, sc2, sh2, w_out, ln_g, ln_b, w_r, b_r, *, alpha, ng, epg):
    nb, s, d = x.shape
    mix = sum(part.shape[2] for part in mix_parts)
    mrows = g1.shape[1]
    ts = min(s, 512)
    assert s % ts == 0
    tok = lambda b, i: (b, i, 0)
    mod = lambda b, i: (b, 0, 0)
    const = lambda b, i: (0, 0)
    return pl.pallas_call(
        functools.partial(_outproj_kernel, n_parts=len(mix_parts), alpha=alpha, ng=ng, epg=epg),
        out_shape=(jax.ShapeDtypeStruct((nb, s, d), F32),
                   jax.ShapeDtypeStruct((nb, s, d), BF16),
                   jax.ShapeDtypeStruct((nb, s, V7X_LANES), F32)),
        grid=(nb, s // ts),
        in_specs=[pl.BlockSpec((1, ts, part.shape[2]), tok) for part in mix_parts] + [
                  pl.BlockSpec((1, ts, d), tok),
                  pl.BlockSpec((1, mrows, d), mod),
                  pl.BlockSpec((1, mrows, d), mod),
                  pl.BlockSpec((1, mrows, d), mod),
                  pl.BlockSpec((mix, d), const),
                  pl.BlockSpec((1, d), const),
                  pl.BlockSpec((1, d), const),
                  pl.BlockSpec((3 * d, V7X_LANES), const),
                  pl.BlockSpec((1, V7X_LANES), const)],
        out_specs=(pl.BlockSpec((1, ts, d), tok),
                   pl.BlockSpec((1, ts, d), tok),
                   pl.BlockSpec((1, ts, V7X_LANES), tok)),
        compiler_params=pltpu.CompilerParams(
            dimension_semantics=("arbitrary", "arbitrary"),
            vmem_limit_bytes=V7X_VMEM_BUDGET),
        name="outproj_router",
    )(*mix_parts, x, g1, sc2, sh2, w_out, ln_g, ln_b, w_r, b_r)


def _moe_kernel(h_ref, gate_ref, x1_ref, g2_ref, wg_ref, wu_ref, wd_ref, lg_ref, lb_ref,
                y_ref, acc_ref, *, alpha, ng, ep):
    e = pl.program_id(2)

    @pl.when(e == 0)
    def _():
        acc_ref[...] = jnp.zeros_like(acc_ref)

    h = h_ref[0]
    gates = gate_ref[0]
    lane = lax.broadcasted_iota(jnp.int32, gates.shape, 1)
    zs = []
    for i in range(ep):
        a = _dot(h, wg_ref[i])
        u = _dot(h, wu_ref[i])
        gcol = jnp.sum(jnp.where(lane == ng + e * ep + i, gates, 0.0), axis=-1, keepdims=True)
        zs.append((a * (1.0 / (1.0 + jnp.exp(-a))) * u * gcol).astype(BF16))
    zz = zs[0] if ep == 1 else jnp.concatenate(zs, axis=-1)
    acc_ref[...] += _dot(zz, wd_ref[...])

    @pl.when(e == pl.num_programs(2) - 1)
    def _():
        y_ref[0] = _layer_norm(alpha * x1_ref[0] + g2_ref[0] * acc_ref[...],
                               lg_ref[...], lb_ref[...])


def _moe(h2, gates, x1, g2, w_gate, w_up, w_down2, ln_g, ln_b, *, alpha, ng):
    nb, s, d = x1.shape
    ne, _, f = w_gate.shape
    mrows = g2.shape[1]
    tm = min(s, 1024)
    assert s % tm == 0
    ep = 4 if ne % 4 == 0 else 1
    tok = lambda b, i, e: (b, i, 0)
    mod = lambda b, i, e: (b, 0, 0)
    const = lambda b, i, e: (0, 0)
    return pl.pallas_call(
        functools.partial(_moe_kernel, alpha=alpha, ng=ng, ep=ep),
        out_shape=jax.ShapeDtypeStruct((nb, s, d), F32),
        grid=(nb, s // tm, ne // ep),
        in_specs=[pl.BlockSpec((1, tm, d), tok),
                  pl.BlockSpec((1, tm, V7X_LANES), tok),
                  pl.BlockSpec((1, tm, d), tok),
                  pl.BlockSpec((1, mrows, d), mod),
                  pl.BlockSpec((ep, d, f), lambda b, i, e: (e, 0, 0)),
                  pl.BlockSpec((ep, d, f), lambda b, i, e: (e, 0, 0)),
                  pl.BlockSpec((ep * f, d), lambda b, i, e: (e, 0)),
                  pl.BlockSpec((1, d), const),
                  pl.BlockSpec((1, d), const)],
        out_specs=pl.BlockSpec((1, tm, d), tok),
        scratch_shapes=[pltpu.VMEM((tm, d), F32)],
        compiler_params=pltpu.CompilerParams(
            dimension_semantics=("arbitrary", "arbitrary", "arbitrary"),
            vmem_limit_bytes=V7X_VMEM_BUDGET),
        name="moe_experts",
    )(h2, gates, x1, g2, w_gate, w_up, w_down2, ln_g, ln_b)


def _pad_to(a, axis, size):
    if a.shape[axis] == size:
        return a
    pad = [(0, 0)] * a.ndim
    pad[axis] = (0, size - a.shape[axis])
    return jnp.pad(a, pad)


def kernel(x_prompt, x_sample, c_prompt, c_sample, cache_k, cache_v, cache_logf, page_table, w_in, b_forget, g_mix, w_out, w_ada, b_ada, ln1_g, ln1_b, ln2_g, ln2_b, w_router_group, b_router_group, w_router_expert, b_router_expert, w_gate_e, w_up_e, w_down_e):
    b, s, d = x_prompt.shape
    db, n_tok, _ = x_sample.shape
    depth = w_in.shape[0]
    n_pool, _, ps, nh, hd = cache_k.shape
    nf = b_forget.shape[1]
    mix = nh * hd
    ng = w_router_group.shape[2]
    ne = w_router_expert.shape[2]
    epg = ne // ng
    f = w_gate_e.shape[3]
    nfp = _round_up(nf, V7X_SUBLANES)
    alpha = float((2 * depth) ** 0.25)
    assert nf % 2 == 0 and (nh - nf) % 2 == 0 and 2 * hd == V7X_LANES
    assert ng + ne <= V7X_LANES and n_tok <= ps

    cache_kt = jnp.transpose(cache_k, (0, 1, 3, 4, 2)).reshape(n_pool, depth, mix, ps)
    cache_vt = jnp.transpose(cache_v, (0, 1, 3, 4, 2)).reshape(n_pool, depth, mix, ps)
    cache_lf_t = _pad_to(jnp.swapaxes(cache_logf, 2, 3), 2, nfp)

    n_c = b + db
    c_all = _pad_to(jnp.concatenate([c_prompt, c_sample], axis=0), 0, _round_up(n_c, V7X_SUBLANES))

    yp, ys = x_prompt, x_sample.reshape(1, db * n_tok, d)
    outs = {name: [] for name in ("kp", "vp", "lp", "ks", "vs", "ls")}
    for l in range(depth):
        mods = _adaln(c_all, w_ada[l], b_ada[l])
        mods_p = [m.reshape(b, 1, d) for m in jnp.split(mods[:b], 6, axis=-1)]
        mods_s = [jnp.repeat(m, n_tok, axis=0).reshape(1, db * n_tok, d)
                  for m in jnp.split(mods[b:n_c], 6, axis=-1)]

        w_qkv = w_in[l][:, :3 * mix].astype(BF16)
        w_f = w_in[l][:, 3 * mix:]
        wf = _pad_to(w_f, 1, V7X_LANES).astype(BF16)
        wft = _pad_to(w_f.T, 0, nfp).astype(BF16)
        bf = b_forget[l].reshape(1, nf)
        bft = _pad_to(b_forget[l].reshape(nf, 1), 0, nfp)
        g_mix2 = g_mix[l].reshape(1, mix)
        w_out_b = w_out[l].astype(BF16)
        w_r = _pad_to(jnp.concatenate([w_router_group[l], w_router_expert[l]], axis=1), 1, V7X_LANES)
        w_r_hi, w_r_lo = _split2(w_r)
        w_r = jnp.concatenate([w_r_hi, w_r_hi, w_r_lo], axis=0)
        b_r = _pad_to(jnp.concatenate([b_router_group[l], b_router_expert[l]]).reshape(1, ng + ne),
                      1, V7X_LANES)
        wg_b = w_gate_e[l].astype(BF16)
        wu_b = w_up_e[l].astype(BF16)
        wd_b = w_down_e[l].astype(BF16).reshape(ne * f, d)
        lg1, lb1 = ln1_g[l].reshape(1, d), ln1_b[l].reshape(1, d)
        lg2, lb2 = ln2_g[l].reshape(1, d), ln2_b[l].reshape(1, d)
        proj = functools.partial(_inproj, w_qkv=w_qkv, wf=wf, wft=wft, bf=bf, bft=bft,
                                 mix=mix, nf=nf, hd=hd)
        ffn = functools.partial(_moe, w_gate=wg_b, w_up=wu_b, w_down2=wd_b, ln_g=lg2, ln_b=lb2,
                                alpha=alpha, ng=ng)
        mixer_out = functools.partial(_outproj, w_out=w_out_b, ln_g=lg1, ln_b=lb1, w_r=w_r, b_r=b_r,
                                      alpha=alpha, ng=ng, epg=epg)

        sh1, sc1, g1, sh2, sc2, g2 = mods_p
        q, k, v, kb, vb, lf, cum, q_t, v_t = proj(yp, sc1, sh1)
        cum4 = cum[:, :nf].reshape(b, nf // 2, 2, s)
        mixn = _prompt_attention(q, kb, vb, q_t, v_t, cum4, g_mix2, nf=nf, hd=hd)
        x1, h2, gates = mixer_out(mixn, yp, g1, sc2, sh2)
        yp = ffn(h2, gates, x1, g2)
        outs["kp"].append(k.reshape(b, s, nh, hd))
        outs["vp"].append(v.reshape(b, s, nh, hd))
        outs["lp"].append(lf)

        sh1, sc1, g1, sh2, sc2, g2 = mods_s
        q, k, v, kb, vb, lf, _, _, _ = proj(ys, sc1, sh1)
        q_s = q.reshape(db, n_tok, mix).astype(F32)
        k_new_t = _pad_to(jnp.swapaxes(kb.reshape(db, n_tok, mix), 1, 2), 2, ps)
        v_new_t = _pad_to(jnp.swapaxes(vb.reshape(db, n_tok, mix), 1, 2), 2, ps)
        lf_new_t = _pad_to(_pad_to(jnp.swapaxes(lf.reshape(db, n_tok, nf), 1, 2), 1, nfp), 2, ps)
        mixn = _decode_attention(q_s, k_new_t, v_new_t, lf_new_t, cache_kt, cache_vt, cache_lf_t,
                                 page_table, g_mix2, layer=l, nh=nh, nf=nf, hd=hd)
        x1, h2, gates = mixer_out((mixn.reshape(1, db * n_tok, mix),), ys, g1, sc2, sh2)
        ys = ffn(h2, gates, x1, g2)
        outs["ks"].append(k.reshape(db, n_tok, nh, hd))
        outs["vs"].append(v.reshape(db, n_tok, nh, hd))
        outs["ls"].append(lf.reshape(db, n_tok, nf))

    stack = lambda name: jnp.stack(outs[name], axis=1)
    return (yp, ys.reshape(db, n_tok, d), stack("kp"), stack("vp"), stack("lp"),
            stack("ks"), stack("vs"), stack("ls"))
```

```python
import functools

import jax
import jax.numpy as jnp
from jax import lax
from jax.experimental import pallas as pl
from jax.experimental.pallas import tpu as pltpu

F32 = jnp.float32
BF16 = jnp.bfloat16
LN_EPS = 1e-5
RMS_EPS = 1e-6
LOG2E = 1.4426950408889634
SB_DEAD_MASS = 104.0
FOX_FOLD_PARTS = 3

V7X_LANES = 128
V7X_SUBLANES = 8
V7X_MXU_DIM = 256
V7X_VMEM_BUDGET = 56 * 1024 * 1024

_NT = (((1,), (1,)), ((), ()))


def _dot(a, b):
    return jnp.dot(a, b, preferred_element_type=F32)


def _dot_nt(a, b):
    return lax.dot_general(a, b, _NT, preferred_element_type=F32)


def _round_up(n, m):
    return (n + m - 1) // m * m


def _split3(x):
    hi = x.astype(BF16)
    r1 = x - hi.astype(F32)
    mid = r1.astype(BF16)
    lo = (r1 - mid.astype(F32)).astype(BF16)
    return hi, mid, lo


def _split2(x):
    hi = x.astype(BF16)
    lo = (x - hi.astype(F32)).astype(BF16)
    return hi, lo


def _dot_parts(parts, tri_stack):
    width = tri_stack.shape[1]
    return _dot(jnp.concatenate(parts, axis=-1), tri_stack[0:len(parts) * width])


def _stack_rows(tri, n):
    return jnp.concatenate([tri] * n, axis=0)


def _softplus_neg_abs(z):
    return jnp.log(1.0 + jnp.exp2(jnp.abs(z) * (-LOG2E)))


def _layer_norm(x, g, b):
    mu = jnp.mean(x, axis=-1, keepdims=True)
    xc = x - mu
    var = jnp.mean(xc * xc, axis=-1, keepdims=True)
    return xc * lax.rsqrt(var + LN_EPS) * g + b


def _adaln_kernel(c_ref, w_ref, b_ref, o_ref):
    c = c_ref[...]
    sc = c * (1.0 / (1.0 + jnp.exp(-c)))
    o_ref[...] = jnp.dot(sc, w_ref[...], precision=lax.Precision.HIGHEST,
                         preferred_element_type=F32) + b_ref[...]


def _adaln(c_all, w_ada, b_ada):
    rows, d = c_all.shape
    n = w_ada.shape[1]
    tn = min(n, 512)
    return pl.pallas_call(
        _adaln_kernel,
        out_shape=jax.ShapeDtypeStruct((rows, n), F32),
        grid=(n // tn,),
        in_specs=[pl.BlockSpec((rows, d), lambda j: (0, 0)),
                  pl.BlockSpec((d, tn), lambda j: (0, j)),
                  pl.BlockSpec((1, tn), lambda j: (0, j))],
        out_specs=pl.BlockSpec((rows, tn), lambda j: (0, j)),
        compiler_params=pltpu.CompilerParams(dimension_semantics=("arbitrary",)),
        name="adaln",
    )(c_all, w_ada, b_ada.reshape(1, n))


def _inproj_kernel(x_ref, sc_ref, sh_ref, w_ref, wf_ref, wft_ref, bf_ref, bft_ref, tri_ref, qs_ref,
                   q_ref, k_ref, v_ref, kb_ref, vb_ref, lf_ref, cum_ref, qt_ref, vt_ref, carry_ref,
                   *, mix, nf):
    si = pl.program_id(1)
    fox_w = qt_ref.shape[1]
    h = (x_ref[0] * (1.0 + sc_ref[0]) + sh_ref[0]).astype(BF16)
    q = _dot(h, w_ref[:, 0:mix]) * qs_ref[...]
    q_ref[0] = q.astype(BF16)
    qt_ref[0] = q[:, 0:fox_w].T.astype(BF16)
    k = _dot(h, w_ref[:, mix:2 * mix])
    k_ref[0] = k
    kb_ref[0] = k.astype(BF16)
    v = _dot(h, w_ref[:, 2 * mix:3 * mix])
    v_ref[0] = v
    vb_ref[0] = v.astype(BF16)
    vt_ref[0] = v[:, 0:fox_w].T.astype(BF16)

    logit = _dot(h, wf_ref[...])[:, 0:nf] + bf_ref[...]
    lf_ref[0] = jnp.minimum(logit, 0.0) - _softplus_neg_abs(logit)

    logit_t = _dot_nt(wft_ref[...], h) + bft_ref[...]
    lf_t = jnp.minimum(logit_t, 0.0) - _softplus_neg_abs(logit_t)
    local = _dot_parts(_split3(lf_t), tri_ref[...])

    @pl.when(si == 0)
    def _():
        carry_ref[...] = jnp.zeros_like(carry_ref)

    cum = local + carry_ref[:, 0:1]
    cum_ref[0] = cum
    ts = cum.shape[1]
    carry_ref[...] = jnp.broadcast_to(cum[:, ts - 1:ts], carry_ref.shape)


def _inproj(x, sc, sh, w_qkv, wf, wft, bf, bft, *, mix, nf, hd):
    nb, s, d = x.shape
    mrows = sc.shape[1]
    ts = min(s, 512)
    assert s % ts == 0
    nfp = wft.shape[0]
    tri = _stack_rows((jnp.arange(ts)[:, None] <= jnp.arange(ts)[None, :]).astype(BF16), 3)
    col_head = jnp.arange(mix) // hd
    q_scale = (jnp.where(col_head < nf, LOG2E, 1.0) * float(hd) ** -0.5).astype(F32).reshape(1, mix)
    tok = lambda b, i: (b, i, 0)
    mod = lambda b, i: (b, 0, 0)
    const = lambda b, i: (0, 0)
    return pl.pallas_call(
        functools.partial(_inproj_kernel, mix=mix, nf=nf),
        out_shape=(jax.ShapeDtypeStruct((nb, s, mix), BF16),
                   jax.ShapeDtypeStruct((nb, s, mix), F32),
                   jax.ShapeDtypeStruct((nb, s, mix), F32),
                   jax.ShapeDtypeStruct((nb, s, mix), BF16),
                   jax.ShapeDtypeStruct((nb, s, mix), BF16),
                   jax.ShapeDtypeStruct((nb, s, nf), F32),
                   jax.ShapeDtypeStruct((nb, nfp, s), F32),
                   jax.ShapeDtypeStruct((nb, nf * hd, s), BF16),
                   jax.ShapeDtypeStruct((nb, nf * hd, s), BF16)),
        grid=(nb, s // ts),
        in_specs=[pl.BlockSpec((1, ts, d), tok),
                  pl.BlockSpec((1, mrows, d), mod),
                  pl.BlockSpec((1, mrows, d), mod),
                  pl.BlockSpec((d, 3 * mix), const),
                  pl.BlockSpec((d, V7X_LANES), const),
                  pl.BlockSpec((nfp, d), const),
                  pl.BlockSpec((1, nf), const),
                  pl.BlockSpec((nfp, 1), const),
                  pl.BlockSpec((3 * ts, ts), const),
                  pl.BlockSpec((1, mix), const)],
        out_specs=(pl.BlockSpec((1, ts, mix), tok),
                   pl.BlockSpec((1, ts, mix), tok),
                   pl.BlockSpec((1, ts, mix), tok),
                   pl.BlockSpec((1, ts, mix), tok),
                   pl.BlockSpec((1, ts, mix), tok),
                   pl.BlockSpec((1, ts, nf), tok),
                   pl.BlockSpec((1, nfp, ts), lambda b, i: (b, 0, i)),
                   pl.BlockSpec((1, nf * hd, ts), lambda b, i: (b, 0, i)),
                   pl.BlockSpec((1, nf * hd, ts), lambda b, i: (b, 0, i))),
        scratch_shapes=[pltpu.VMEM((nfp, V7X_LANES), F32)],
        compiler_params=pltpu.CompilerParams(
            dimension_semantics=("arbitrary", "arbitrary"),
            vmem_limit_bytes=V7X_VMEM_BUDGET),
        name="inproj",
    )(x, sc, sh, w_qkv, wf, wft, bf, bft, tri, q_scale)


def _pair_rmsnorm(outs, lane, hd, g):
    first = (lane // hd) == 0
    o = jnp.where(first, outs[0], outs[1])
    osq = o * o
    ms0 = jnp.sum(jnp.where(first, osq, 0.0), axis=-1, keepdims=True)
    ms1 = jnp.sum(jnp.where(first, 0.0, osq), axis=-1, keepdims=True)
    ms = jnp.where(first, ms0, ms1) * (1.0 / hd)
    return o * lax.rsqrt(ms + RMS_EPS) * g


def _fox_kernel(qt_ref, k_ref, vt_ref, cum_ref, g_ref, o_ref, ka_ref, va_ref, *, tq, hd, chunk):
    qi = pl.program_id(2)
    s_len = k_ref.shape[1]
    lane = lax.broadcasted_iota(jnp.int32, (1, V7X_LANES), 1)
    feat = lax.broadcasted_iota(jnp.int32, (V7X_LANES, 1), 0)

    @pl.when(qi == 0)
    def _():
        for c in range(s_len // chunk):
            rows = slice(c * chunk, (c + 1) * chunk)
            k = k_ref[0, rows, :].astype(F32)
            vt = vt_ref[0, :, rows].astype(F32)
            for h in range(2):
                spare = (1 - h) * hd
                ka = jnp.where((lane // hd) == h, k, 0.0)
                for i, part in enumerate(_split3(cum_ref[0, 0, rows, h:h + 1] * LOG2E)):
                    ka = jnp.where(lane == spare + i, part.astype(F32), ka)
                ka_ref[h, rows, :] = ka.astype(BF16)
                va_ref[h, :, rows] = jnp.where((feat // hd) == h, vt, 1.0).astype(BF16)

    featq = lax.broadcasted_iota(jnp.int32, (V7X_LANES, tq), 0)
    qtf = qt_ref[0].astype(F32)
    qa = []
    for h in range(2):
        spare = (1 - h) * hd
        fold = (featq >= spare) & (featq < spare + FOX_FOLD_PARTS)
        qa.append(jnp.where((featq // hd) == h, qtf, jnp.where(fold, -1.0, 0.0)).astype(BF16))
    krow = lax.broadcasted_iota(jnp.int32, (tq, tq), 0)
    qcol = lax.broadcasted_iota(jnp.int32, (tq, tq), 1)

    def step(kb, carry, masked):
        k0 = pl.multiple_of(kb * tq, tq)
        new = []
        for h in range(2):
            m, acc = carry[h]
            s = _dot(ka_ref[h, pl.ds(k0, tq), :], qa[h])
            if masked:
                s = jnp.where(krow <= qcol, s, -jnp.inf)
            m_new = jnp.maximum(m, jnp.max(s, axis=0, keepdims=True))
            alpha = jnp.exp2(m - m_new)
            p = jnp.exp2(s - m_new).astype(BF16)
            new.append((m_new, alpha * acc + _dot(va_ref[h, :, pl.ds(k0, tq)], p)))
        return tuple(new)

    init = tuple((jnp.full((1, tq), -jnp.inf, F32), jnp.zeros((V7X_LANES, tq), F32)) for _ in range(2))
    carry = lax.fori_loop(0, qi, functools.partial(step, masked=False), init)
    carry = step(qi, carry, True)
    first = (feat // hd) == 0
    den0 = carry[0][1][hd:hd + 1, :]
    den1 = carry[1][1][0:1, :]
    o = jnp.where(first, carry[0][1] / den0, carry[1][1] / den1)
    osq = o * o
    ms0 = jnp.sum(jnp.where(first, osq, 0.0), axis=0, keepdims=True)
    ms1 = jnp.sum(jnp.where(first, 0.0, osq), axis=0, keepdims=True)
    ms = jnp.where(first, ms0, ms1) * (1.0 / hd)
    o_ref[0] = (o * lax.rsqrt(ms + RMS_EPS) * g_ref[...]).T.astype(BF16)


def _sb_kernel(q_ref, k_ref, v_ref, tri_ref, g_ref, o_ref, *, tq, sub, hd):
    qi = pl.program_id(2)
    q = q_ref[0]
    lane = lax.broadcasted_iota(jnp.int32, (1, V7X_LANES), 1)
    row = lax.broadcasted_iota(jnp.int32, (tq, tq), 0)
    col = lax.broadcasted_iota(jnp.int32, (tq, tq), 1)
    qh = [jnp.where((lane // hd) == h, q, jnp.zeros_like(q)) for h in range(2)]

    def step(k0, nk, carry, qs, masked):
        n_sub = nk // sub
        k = k_ref[0, pl.ds(k0, nk), :]
        v = v_ref[0, pl.ds(k0, nk), :]
        tri = tri_ref[...]
        new = []
        for h in range(2):
            rsum, acc = carry[h]
            z = _dot_nt(qs[h], k)
            sp = jnp.maximum(z, 0.0) + _softplus_neg_abs(z)
            lsz = z - sp
            if masked:
                valid = col < row
                sp = jnp.where(valid, sp, 0.0)
            afters = []
            later = rsum
            for c in reversed(range(n_sub)):
                part = sp[:, c * sub:(c + 1) * sub]
                afters.append(_dot(part.astype(BF16), tri) + later)
                later = later + jnp.sum(part, axis=-1, keepdims=True)
            after = afters[0] if n_sub == 1 else jnp.concatenate(afters[::-1], axis=-1)
            a = jnp.exp(lsz - after)
            if masked:
                a = jnp.where(valid, a, 0.0)
            new.append((later, acc + _dot(a.astype(BF16), v)))
        return tuple(new)

    def least_mass(carry, first_row=0):
        return jnp.minimum(jnp.min(carry[0][0][first_row:]), jnp.min(carry[1][0][first_row:]))

    def walk_earlier(start, nk, carry, qs, first_row=0):
        n_units = qi * (tq // nk)

        def more_to_do(state):
            i, _, least = state
            return jnp.logical_and(i < n_units, least < SB_DEAD_MASS)

        def earlier_unit(state):
            i, c, _ = state
            c = step(pl.multiple_of(qi * tq - (i + 1) * nk, nk), nk, c, qs, False)
            return i + 1, c, least_mass(c, first_row)

        i, carry, _ = lax.while_loop(more_to_do, earlier_unit, (start, carry, least_mass(carry, first_row)))
        return i, carry

    init = tuple((jnp.zeros((tq, 1), F32), jnp.zeros((tq, V7X_LANES), F32)) for _ in range(2))
    carry = step(pl.multiple_of(qi * tq, tq), tq, init, qh, True)
    top = tq // 2
    i, carry = walk_earlier(jnp.int32(0), tq, carry, qh, first_row=top)
    _, upper = walk_earlier(i * (tq // sub), sub, tuple((r[:top], a[:top]) for (r, a) in carry),
                            [x[:top] for x in qh])
    carry = tuple((r, jnp.concatenate([ua, a[top:]], axis=0)) for (_, ua), (r, a) in zip(upper, carry))
    outs = [acc for (_, acc) in carry]
    o_ref[0] = _pair_rmsnorm(outs, lane, hd, g_ref[...]).astype(BF16)


def _prompt_attention(q, kb, vb, q_t, v_t, cum4, g_mix2, *, nf, hd):
    b, s, mix = q.shape
    nh = mix // hd
    fox_pairs = nf // 2
    sb_pairs = (nh - nf) // 2
    tq_fox = min(s, 1024)
    tq_sb = min(s, 512)
    sub = min(tq_sb, V7X_MXU_DIM)
    assert s % tq_fox == 0 and s % tq_sb == 0 and tq_sb % sub == 0
    tri = (jnp.arange(sub)[:, None] > jnp.arange(sub)[None, :]).astype(BF16)
    params = pltpu.CompilerParams(
        dimension_semantics=("arbitrary", "arbitrary", "arbitrary"),
        vmem_limit_bytes=V7X_VMEM_BUDGET)

    def specs(off, tq):
        return (pl.BlockSpec((1, tq, V7X_LANES), lambda bi, p, i: (bi, i, p + off)),
                pl.BlockSpec((1, s, V7X_LANES), lambda bi, p, i: (bi, 0, p + off)),
                pl.BlockSpec((1, s, V7X_LANES), lambda bi, p, i: (bi, 0, p + off)),
                pl.BlockSpec((1, V7X_LANES), lambda bi, p, i: (0, p + off)),
                pl.BlockSpec((1, tq, V7X_LANES), lambda bi, p, i: (bi, i, p)))

    fox_w = fox_pairs * V7X_LANES
    chunk = min(s, 512)
    assert s % chunk == 0
    cum_col = jnp.swapaxes(cum4, 2, 3)
    o_fox = pl.pallas_call(
        functools.partial(_fox_kernel, tq=tq_fox, hd=hd, chunk=chunk),
        out_shape=jax.ShapeDtypeStruct((b, s, fox_w), BF16),
        grid=(b, fox_pairs, s // tq_fox),
        in_specs=[pl.BlockSpec((1, V7X_LANES, tq_fox), lambda bi, p, i: (bi, p, i)),
                  pl.BlockSpec((1, s, V7X_LANES), lambda bi, p, i: (bi, 0, p)),
                  pl.BlockSpec((1, V7X_LANES, s), lambda bi, p, i: (bi, p, 0)),
                  pl.BlockSpec((1, 1, s, 2), lambda bi, p, i: (bi, p, 0, 0)),
                  pl.BlockSpec((V7X_LANES, 1), lambda bi, p, i: (p, 0))],
        out_specs=pl.BlockSpec((1, tq_fox, V7X_LANES), lambda bi, p, i: (bi, i, p)),
        scratch_shapes=[pltpu.VMEM((2, s, V7X_LANES), BF16), pltpu.VMEM((2, V7X_LANES, s), BF16)],
        compiler_params=params,
        name="fox_prompt",
    )(q_t, kb, v_t, cum_col, g_mix2.reshape(mix, 1))
    qs, ks, vs, gs, os_ = specs(fox_pairs, tq_sb)
    o_sb = pl.pallas_call(
        functools.partial(_sb_kernel, tq=tq_sb, sub=sub, hd=hd),
        out_shape=jax.ShapeDtypeStruct((b, s, sb_pairs * V7X_LANES), BF16),
        grid=(b, sb_pairs, s // tq_sb),
        in_specs=[qs, ks, vs, pl.BlockSpec((sub, sub), lambda bi, p, i: (0, 0)), gs],
        out_specs=os_,
        compiler_params=params,
        name="sb_prompt",
    )(q, kb, vb, tri, g_mix2)
    return o_fox, o_sb


def _decode_kernel(pt_ref, q_ref, knt_ref, vnt_ref, lfn_ref, tri_ref, g_ref, *rest,
                   pps, n_tok, nh, nf, hd):
    del pt_ref
    k_refs = rest[0:pps]
    v_refs = rest[pps:2 * pps]
    lf_refs = rest[2 * pps:3 * pps]
    o_ref = rest[3 * pps]
    qbd_ref, m_ref, l_ref, r_ref, gc_ref, acc_ref = rest[3 * pps + 1:]
    j = pl.program_id(1)
    rows = n_tok * nh
    ps = tri_ref.shape[1]
    mix = nh * hd
    rowi = lax.broadcasted_iota(jnp.int32, (rows, 1), 0)
    row_h = rowi % nh
    row_t = rowi // nh
    is_fox = row_h < nf
    key = lax.broadcasted_iota(jnp.int32, (1, ps), 1)
    lane = lax.broadcasted_iota(jnp.int32, (1, mix), 1)

    def process(blocks, new):
        tri = tri_ref[...]
        qbd = qbd_ref[...]
        g_off = gc_ref[0:nf, 0:1]
        r_off = r_ref[:, 0:1]
        xs, a_sbs = [], []
        for k_t, _, lf_t in blocks:
            s = _dot(qbd, k_t)
            lf = lf_t[0:nf]
            g_suffix = (_dot_parts(_split3(lf), tri) + g_off) * LOG2E
            g_off = g_off + jnp.sum(lf, axis=-1, keepdims=True)
            if nh > nf:
                g_suffix = jnp.concatenate([g_suffix, jnp.zeros((nh - nf, ps), F32)], axis=0)
            x = s + jnp.concatenate([g_suffix] * n_tok, axis=0)
            if new:
                x = jnp.where((key < n_tok) & (key <= row_t), x, -jnp.inf)
            xs.append(x)
            sp = jnp.maximum(s, 0.0) + _softplus_neg_abs(s)
            lsz = s - sp
            if new:
                valid = key < row_t
                sp = jnp.where(valid, sp, 0.0)
            a_sb = jnp.exp(lsz - (_dot(sp.astype(BF16), tri[0:ps]) + r_off))
            if new:
                a_sb = jnp.where(valid, a_sb, 0.0)
            a_sbs.append(a_sb)
            r_off = r_off + jnp.sum(sp, axis=-1, keepdims=True)
        gc_ref[0:nf, :] = jnp.broadcast_to(g_off, (nf, gc_ref.shape[1]))
        r_ref[...] = jnp.broadcast_to(r_off, r_ref.shape)
        m_prev = m_ref[:, 0:1]
        m_new = m_prev
        for x in xs:
            m_new = jnp.maximum(m_new, jnp.max(x, axis=-1, keepdims=True))
        alpha = jnp.exp2(m_prev - m_new)
        l_new = alpha * l_ref[:, 0:1]
        pv = None
        for x, a_sb, (_, v_t, _) in zip(xs, a_sbs, blocks):
            p_fox = jnp.exp2(x - m_new)
            l_new = l_new + jnp.sum(p_fox, axis=-1, keepdims=True)
            d = _dot_nt(jnp.where(is_fox, p_fox, a_sb).astype(BF16), v_t)
            pv = d if pv is None else pv + d
        m_ref[...] = jnp.broadcast_to(m_new, m_ref.shape)
        l_ref[...] = jnp.broadcast_to(l_new, l_ref.shape)
        acc_ref[...] = acc_ref[...] * jnp.where(is_fox, alpha, 1.0) + pv

    @pl.when(j == 0)
    def _():
        q = q_ref[0]
        head_rows = lax.broadcasted_iota(jnp.int32, (nh, 1), 0)
        own = (lane // hd) == head_rows
        qbd_ref[...] = jnp.concatenate(
            [jnp.where(own, jnp.broadcast_to(q[t:t + 1], (nh, mix)), 0.0) for t in range(n_tok)],
            axis=0).astype(BF16)
        m_ref[...] = jnp.full(m_ref.shape, -jnp.inf, F32)
        l_ref[...] = jnp.zeros_like(l_ref)
        r_ref[...] = jnp.zeros_like(r_ref)
        gc_ref[...] = jnp.zeros_like(gc_ref)
        acc_ref[...] = jnp.zeros_like(acc_ref)
        process([(knt_ref[0], vnt_ref[0], lfn_ref[0])], True)

    process([(k_refs[i][0, 0].astype(BF16), v_refs[i][0, 0].astype(BF16), lf_refs[i][0, 0])
             for i in range(pps)], False)

    @pl.when(j == pl.num_programs(1) - 1)
    def _():
        o = acc_ref[...] * jnp.where(is_fox, 1.0 / l_ref[:, 0:1], 1.0)
        own = (lane // hd) == row_h
        ms = jnp.sum(jnp.where(own, o * o, 0.0), axis=-1, keepdims=True) * (1.0 / hd)
        on = jnp.where(own, o * lax.rsqrt(ms + RMS_EPS), 0.0)
        outs = [jnp.sum(on[t * nh:(t + 1) * nh], axis=0, keepdims=True) for t in range(n_tok)]
        o_ref[0] = (jnp.concatenate(outs, axis=0) * g_ref[...]).astype(BF16)


def _decode_attention(q, k_new_t, v_new_t, lf_new_t, cache_kt, cache_vt, cache_lf_t, page_table,
                      g_mix2, *, layer, nh, nf, hd):
    db, n_tok, mix = q.shape
    n_pages = page_table.shape[1]
    nfp, ps = cache_lf_t.shape[2:]
    pps = max(p for p in (16, 8, 4, 2, 1) if n_pages % p == 0)
    rows = n_tok * nh
    tri = _stack_rows((jnp.arange(ps)[:, None] > jnp.arange(ps)[None, :]).astype(BF16), 3)
    per_b = lambda b, j, pt: (b, 0, 0)
    const = lambda b, j, pt: (0, 0)

    def page_map(i):
        return lambda b, j, pt: (pt[b, n_pages - 1 - (j * pps + i)], layer, 0, 0)

    in_specs = [pl.BlockSpec((1, n_tok, mix), per_b),
                pl.BlockSpec((1, mix, ps), per_b),
                pl.BlockSpec((1, mix, ps), per_b),
                pl.BlockSpec((1, nfp, ps), per_b),
                pl.BlockSpec((3 * ps, ps), const),
                pl.BlockSpec((1, mix), const)]
    in_specs += [pl.BlockSpec((1, 1, mix, ps), page_map(i)) for i in range(pps)]
    in_specs += [pl.BlockSpec((1, 1, mix, ps), page_map(i)) for i in range(pps)]
    in_specs += [pl.BlockSpec((1, 1, nfp, ps), page_map(i)) for i in range(pps)]
    grid_spec = pltpu.PrefetchScalarGridSpec(
        num_scalar_prefetch=1,
        grid=(db, n_pages // pps),
        in_specs=in_specs,
        out_specs=pl.BlockSpec((1, n_tok, mix), per_b),
        scratch_shapes=[pltpu.VMEM((rows, mix), BF16),
                        pltpu.VMEM((rows, V7X_LANES), F32),
                        pltpu.VMEM((rows, V7X_LANES), F32),
                        pltpu.VMEM((rows, V7X_LANES), F32),
                        pltpu.VMEM((nfp, V7X_LANES), F32),
                        pltpu.VMEM((rows, mix), F32)])
    return pl.pallas_call(
        functools.partial(_decode_kernel, pps=pps, n_tok=n_tok, nh=nh, nf=nf, hd=hd),
        out_shape=jax.ShapeDtypeStruct((db, n_tok, mix), BF16),
        grid_spec=grid_spec,
        compiler_params=pltpu.CompilerParams(
            dimension_semantics=("arbitrary", "arbitrary"),
            vmem_limit_bytes=V7X_VMEM_BUDGET),
        name="decode_attn",
    )(page_table, q, k_new_t, v_new_t, lf_new_t, tri, g_mix2,
      *([cache_kt] * pps), *([cache_vt] * pps), *([cache_lf_t] * pps))


def _outproj_kernel(*refs, n_parts, alpha, ng, epg):
    mix_refs = refs[:n_parts]
    (x_ref, g1_ref, sc_ref, sh_ref, w_ref, lg_ref, lb_ref, wr_ref, br_ref,
     x1_ref, h2_ref, gate_ref) = refs[n_parts:]
    mo, off = None, 0
    for part in mix_refs:
        width = part.shape[2]
        term = _dot(part[0], w_ref[off:off + width, :])
        mo = term if mo is None else mo + term
        off += width
    x1 = _layer_norm(alpha * x_ref[0] + g1_ref[0] * mo, lg_ref[...], lb_ref[...])
    x1_ref[0] = x1
    h2 = x1 * (1.0 + sc_ref[0]) + sh_ref[0]
    h2_ref[0] = h2.astype(BF16)
    h_hi, h_lo = _split2(h2)
    logits = _dot(jnp.concatenate([h_hi, h_lo, h_hi], axis=-1), wr_ref[...]) + br_ref[...]
    lane = lax.broadcasted_iota(jnp.int32, logits.shape, 1)
    big = jnp.int32(2 ** 30)
    neg = -jnp.inf
    gl = jnp.where(lane < ng, logits, neg)
    gmax = jnp.max(gl, axis=-1, keepdims=True)
    gidx = jnp.min(jnp.where(gl == gmax, lane, big), axis=-1, keepdims=True)
    g_w = 1.0 / jnp.sum(jnp.exp(gl - gmax), axis=-1, keepdims=True)
    in_group = (lane >= ng + gidx * epg) & (lane < ng + (gidx + 1) * epg)
    el = jnp.where(in_group, logits, neg)
    m1 = jnp.max(el, axis=-1, keepdims=True)
    i1 = jnp.min(jnp.where(el == m1, lane, big), axis=-1, keepdims=True)
    el2 = jnp.where(lane == i1, neg, el)
    m2 = jnp.max(el2, axis=-1, keepdims=True)
    i2 = jnp.min(jnp.where(el2 == m2, lane, big), axis=-1, keepdims=True)
    den = jnp.sum(jnp.exp(el - m1), axis=-1, keepdims=True)
    p1 = 1.0 / den
    p2 = jnp.exp(m2 - m1) / den
    w1 = p1 / (p1 + p2)
    w2 = p2 / (p1 + p2)
    gate_ref[0] = jnp.where(lane == i1, g_w * w1, jnp.where(lane == i2, g_w * w2, 0.0))


def _outproj(mix_parts, x, g1, sc2, sh2, w_out, ln_g, ln_b, w_r, b_r, *, alpha, ng, epg):
    nb, s, d = x.shape
    mix = sum(part.shape[2] for part in mix_parts)
    mrows = g1.shape[1]
    ts = min(s, 512)
    assert s % ts == 0
    tok = lambda b, i: (b, i, 0)
    mod = lambda b, i: (b, 0, 0)
    const = lambda b, i: (0, 0)
    return pl.pallas_call(
        functools.partial(_outproj_kernel, n_parts=len(mix_parts), alpha=alpha, ng=ng, epg=epg),
        out_shape=(jax.ShapeDtypeStruct((nb, s, d), F32),
                   jax.ShapeDtypeStruct((nb, s, d), BF16),
                   jax.ShapeDtypeStruct((nb, s, V7X_LANES), F32)),
        grid=(nb, s // ts),
        in_specs=[pl.BlockSpec((1, ts, part.shape[2]), tok) for part in mix_parts] + [
                  pl.BlockSpec((1, ts, d), tok),
                  pl.BlockSpec((1, mrows, d), mod),
                  pl.BlockSpec((1, mrows, d), mod),
                  pl.BlockSpec((1, mrows, d), mod),
                  pl.BlockSpec((mix, d), const),
                  pl.BlockSpec((1, d), const),
                  pl.BlockSpec((1, d), const),
                  pl.BlockSpec((3 * d, V7X_LANES), const),
                  pl.BlockSpec((1, V7X_LANES), const)],
        out_specs=(pl.BlockSpec((1, ts, d), tok),
                   pl.BlockSpec((1, ts, d), tok),
                   pl.BlockSpec((1, ts, V7X_LANES), tok)),
        compiler_params=pltpu.CompilerParams(
            dimension_semantics=("arbitrary", "arbitrary"),
            vmem_limit_bytes=V7X_VMEM_BUDGET),
        name="outproj_router",
    )(*mix_parts, x, g1, sc2, sh2, w_out, ln_g, ln_b, w_r, b_r)


def _moe_kernel(h_ref, gate_ref, x1_ref, g2_ref, wg_ref, wu_ref, wd_ref, lg_ref, lb_ref,
                y_ref, acc_ref, *, alpha, ng, ep):
    e = pl.program_id(2)

    @pl.when(e == 0)
    def _():
        acc_ref[...] = jnp.zeros_like(acc_ref)

    h = h_ref[0]
    gates = gate_ref[0]
    lane = lax.broadcasted_iota(jnp.int32, gates.shape, 1)
    zs = []
    for i in range(ep):
        a = _dot(h, wg_ref[i])
        u = _dot(h, wu_ref[i])
        gcol = jnp.sum(jnp.where(lane == ng + e * ep + i, gates, 0.0), axis=-1, keepdims=True)
        zs.append((a * (1.0 / (1.0 + jnp.exp(-a))) * u * gcol).astype(BF16))
    zz = zs[0] if ep == 1 else jnp.concatenate(zs, axis=-1)
    acc_ref[...] += _dot(zz, wd_ref[...])

    @pl.when(e == pl.num_programs(2) - 1)
    def _():
        y_ref[0] = _layer_norm(alpha * x1_ref[0] + g2_ref[0] * acc_ref[...],
                               lg_ref[...], lb_ref[...])


def _moe(h2, gates, x1, g2, w_gate, w_up, w_down2, ln_g, ln_b, *, alpha, ng):
    nb, s, d = x1.shape
    ne, _, f = w_gate.shape
    mrows = g2.shape[1]
    tm = min(s, 1024)
    assert s % tm == 0
    ep = 4 if ne % 4 == 0 else 1
    tok = lambda b, i, e: (b, i, 0)
    mod = lambda b, i, e: (b, 0, 0)
    const = lambda b, i, e: (0, 0)
    return pl.pallas_call(
        functools.partial(_moe_kernel, alpha=alpha, ng=ng, ep=ep),
        out_shape=jax.ShapeDtypeStruct((nb, s, d), F32),
        grid=(nb, s // tm, ne // ep),
        in_specs=[pl.BlockSpec((1, tm, d), tok),
                  pl.BlockSpec((1, tm, V7X_LANES), tok),
                  pl.BlockSpec((1, tm, d), tok),
                  pl.BlockSpec((1, mrows, d), mod),
                  pl.BlockSpec((ep, d, f), lambda b, i, e: (e, 0, 0)),
                  pl.BlockSpec((ep, d, f), lambda b, i, e: (e, 0, 0)),
                  pl.BlockSpec((ep * f, d), lambda b, i, e: (e, 0)),
                  pl.BlockSpec((1, d), const),
                  pl.BlockSpec((1, d), const)],
        out_specs=pl.BlockSpec((1, tm, d), tok),
        scratch_shapes=[pltpu.VMEM((tm, d), F32)],
        compiler_params=pltpu.CompilerParams(
            dimension_semantics=("arbitrary", "arbitrary", "arbitrary"),
            vmem_limit_bytes=V7X_VMEM_BUDGET),
        name="moe_experts",
    )(h2, gates, x1, g2, w_gate, w_up, w_down2, ln_g, ln_b)


def _pad_to(a, axis, size):
    if a.shape[axis] == size:
        return a
    pad = [(0, 0)] * a.ndim
    pad[axis] = (0, size - a.shape[axis])
    return jnp.pad(a, pad)


def kernel(x_prompt, x_sample, c_prompt, c_sample, cache_k, cache_v, cache_logf, page_table, w_in, b_forget, g_mix, w_out, w_ada, b_ada, ln1_g, ln1_b, ln2_g, ln2_b, w_router_group, b_router_group, w_router_expert, b_router_expert, w_gate_e, w_up_e, w_down_e):
    b, s, d = x_prompt.shape
    db, n_tok, _ = x_sample.shape
    depth = w_in.shape[0]
    n_pool, _, ps, nh, hd = cache_k.shape
    nf = b_forget.shape[1]
    mix = nh * hd
    ng = w_router_group.shape[2]
    ne = w_router_expert.shape[2]
    epg = ne // ng
    f = w_gate_e.shape[3]
    nfp = _round_up(nf, V7X_SUBLANES)
    alpha = float((2 * depth) ** 0.25)
    assert nf % 2 == 0 and (nh - nf) % 2 == 0 and 2 * hd == V7X_LANES
    assert ng + ne <= V7X_LANES and n_tok <= ps

    cache_kt = jnp.transpose(cache_k, (0, 1, 3, 4, 2)).reshape(n_pool, depth, mix, ps)
    cache_vt = jnp.transpose(cache_v, (0, 1, 3, 4, 2)).reshape(n_pool, depth, mix, ps)
    cache_lf_t = _pad_to(jnp.swapaxes(cache_logf, 2, 3), 2, nfp)

    n_c = b + db
    c_all = _pad_to(jnp.concatenate([c_prompt, c_sample], axis=0), 0, _round_up(n_c, V7X_SUBLANES))

    yp, ys = x_prompt, x_sample.reshape(1, db * n_tok, d)
    outs = {name: [] for name in ("kp", "vp", "lp", "ks", "vs", "ls")}
    for l in range(depth):
        mods = _adaln(c_all, w_ada[l], b_ada[l])
        mods_p = [m.reshape(b, 1, d) for m in jnp.split(mods[:b], 6, axis=-1)]
        mods_s = [jnp.repeat(m, n_tok, axis=0).reshape(1, db * n_tok, d)
                  for m in jnp.split(mods[b:n_c], 6, axis=-1)]

        w_qkv = w_in[l][:, :3 * mix].astype(BF16)
        w_f = w_in[l][:, 3 * mix:]
        wf = _pad_to(w_f, 1, V7X_LANES).astype(BF16)
        wft = _pad_to(w_f.T, 0, nfp).astype(BF16)
        bf = b_forget[l].reshape(1, nf)
        bft = _pad_to(b_forget[l].reshape(nf, 1), 0, nfp)
        g_mix2 = g_mix[l].reshape(1, mix)
        w_out_b = w_out[l].astype(BF16)
        w_r = _pad_to(jnp.concatenate([w_router_group[l], w_router_expert[l]], axis=1), 1, V7X_LANES)
        w_r_hi, w_r_lo = _split2(w_r)
        w_r = jnp.concatenate([w_r_hi, w_r_hi, w_r_lo], axis=0)
        b_r = _pad_to(jnp.concatenate([b_router_group[l], b_router_expert[l]]).reshape(1, ng + ne),
                      1, V7X_LANES)
        wg_b = w_gate_e[l].astype(BF16)
        wu_b = w_up_e[l].astype(BF16)
        wd_b = w_down_e[l].astype(BF16).reshape(ne * f, d)
        lg1, lb1 = ln1_g[l].reshape(1, d), ln1_b[l].reshape(1, d)
        lg2, lb2 = ln2_g[l].reshape(1, d), ln2_b[l].reshape(1, d)
        proj = functools.partial(_inproj, w_qkv=w_qkv, wf=wf, wft=wft, bf=bf, bft=bft,
                                 mix=mix, nf=nf, hd=hd)
        ffn = functools.partial(_moe, w_gate=wg_b, w_up=wu_b, w_down2=wd_b, ln_g=lg2, ln_b=lb2,
                                alpha=alpha, ng=ng)
        mixer_out = functools.partial(_outproj, w_out=w_out_b, ln_g=lg1, ln_b=lb1, w_r=w_r, b_r=b_r,
                                      alpha=alpha, ng=ng, epg=epg)

        sh1, sc1, g1, sh2, sc2, g2 = mods_p
        q, k, v, kb, vb, lf, cum, q_t, v_t = proj(yp, sc1, sh1)
        cum4 = cum[:, :nf].reshape(b, nf // 2, 2, s)
        mixn = _prompt_attention(q, kb, vb, q_t, v_t, cum4, g_mix2, nf=nf, hd=hd)
        x1, h2, gates = mixer_out(mixn, yp, g1, sc2, sh2)
        yp = ffn(h2, gates, x1, g2)
        outs["kp"].append(k.reshape(b, s, nh, hd))
        outs["vp"].append(v.reshape(b, s, nh, hd))
        outs["lp"].append(lf)

        sh1, sc1, g1, sh2, sc2, g2 = mods_s
        q, k, v, kb, vb, lf, _, _, _ = proj(ys, sc1, sh1)
        q_s = q.reshape(db, n_tok, mix).astype(F32)
        k_new_t = _pad_to(jnp.swapaxes(kb.reshape(db, n_tok, mix), 1, 2), 2, ps)
        v_new_t = _pad_to(jnp.swapaxes(vb.reshape(db, n_tok, mix), 1, 2), 2, ps)
        lf_new_t = _pad_to(_pad_to(jnp.swapaxes(lf.reshape(db, n_tok, nf), 1, 2), 1, nfp), 2, ps)
        mixn = _decode_attention(q_s, k_new_t, v_new_t, lf_new_t, cache_kt, cache_vt, cache_lf_t,
                                 page_table, g_mix2, layer=l, nh=nh, nf=nf, hd=hd)
        x1, h2, gates = mixer_out((mixn.reshape(1, db * n_tok, mix),), ys, g1, sc2, sh2)
        ys = ffn(h2, gates, x1, g2)
        outs["ks"].append(k.reshape(db, n_tok, nh, hd))
        outs["vs"].append(v.reshape(db, n_tok, nh, hd))
        outs["ls"].append(lf.reshape(db, n_tok, nf))

    stack = lambda name: jnp.stack(outs[name], axis=1)
    return (yp, ys.reshape(db, n_tok, d), stack("kp"), stack("vp"), stack("lp"),
            stack("ks"), stack("vs"), stack("ls"))
```

```python
import functools

import jax
import jax.numpy as jnp
from jax import lax
from jax.experimental import pallas as pl
from jax.experimental.pallas import tpu as pltpu

F32 = jnp.float32
BF16 = jnp.bfloat16
LN_EPS = 1e-5
RMS_EPS = 1e-6
LOG2E = 1.4426950408889634
SB_DEAD_MASS = 104.0
FOX_FOLD_PARTS = 3

V7X_LANES = 128
V7X_SUBLANES = 8
V7X_MXU_DIM = 256
V7X_VMEM_BUDGET = 56 * 1024 * 1024

_NT = (((1,), (1,)), ((), ()))


def _dot(a, b):
    return jnp.dot(a, b, preferred_element_type=F32)


def _dot_nt(a, b):
    return lax.dot_general(a, b, _NT, preferred_element_type=F32)


def _round_up(n, m):
    return (n + m - 1) // m * m


def _split3(x):
    hi = x.astype(BF16)
    r1 = x - hi.astype(F32)
    mid = r1.astype(BF16)
    lo = (r1 - mid.astype(F32)).astype(BF16)
    return hi, mid, lo


def _split2(x):
    hi = x.astype(BF16)
    lo = (x - hi.astype(F32)).astype(BF16)
    return hi, lo


def _dot_parts(parts, tri_stack):
    width = tri_stack.shape[1]
    return _dot(jnp.concatenate(parts, axis=-1), tri_stack[0:len(parts) * width])


def _stack_rows(tri, n):
    return jnp.concatenate([tri] * n, axis=0)


def _softplus_neg_abs(z):
    return jnp.log(1.0 + jnp.exp2(jnp.abs(z) * (-LOG2E)))


def _layer_norm(x, g, b):
    mu = jnp.mean(x, axis=-1, keepdims=True)
    xc = x - mu
    var = jnp.mean(xc * xc, axis=-1, keepdims=True)
    return xc * lax.rsqrt(var + LN_EPS) * g + b


def _adaln_kernel(c_ref, w_ref, b_ref, o_ref):
    c = c_ref[...]
    sc = c * (1.0 / (1.0 + jnp.exp(-c)))
    o_ref[...] = jnp.dot(sc, w_ref[...], precision=lax.Precision.HIGHEST,
                         preferred_element_type=F32) + b_ref[...]


def _adaln(c_all, w_ada, b_ada):
    rows, d = c_all.shape
    n = w_ada.shape[1]
    tn = min(n, 512)
    return pl.pallas_call(
        _adaln_kernel,
        out_shape=jax.ShapeDtypeStruct((rows, n), F32),
        grid=(n // tn,),
        in_specs=[pl.BlockSpec((rows, d), lambda j: (0, 0)),
                  pl.BlockSpec((d, tn), lambda j: (0, j)),
                  pl.BlockSpec((1, tn), lambda j: (0, j))],
        out_specs=pl.BlockSpec((rows, tn), lambda j: (0, j)),
        compiler_params=pltpu.CompilerParams(dimension_semantics=("arbitrary",)),
        name="adaln",
    )(c_all, w_ada, b_ada.reshape(1, n))


def _inproj_kernel(x_ref, sc_ref, sh_ref, w_ref, wf_ref, wft_ref, bf_ref, bft_ref, tri_ref, qs_ref,
                   q_ref, k_ref, v_ref, kb_ref, vb_ref, lf_ref, cum_ref, qt_ref, vt_ref, carry_ref,
                   *, mix, nf):
    si = pl.program_id(1)
    fox_w = qt_ref.shape[1]
    h = (x_ref[0] * (1.0 + sc_ref[0]) + sh_ref[0]).astype(BF16)
    q = _dot(h, w_ref[:, 0:mix]) * qs_ref[...]
    q_ref[0] = q.astype(BF16)
    qt_ref[0] = q[:, 0:fox_w].T.astype(BF16)
    k = _dot(h, w_ref[:, mix:2 * mix])
    k_ref[0] = k
    kb_ref[0] = k.astype(BF16)
    v = _dot(h, w_ref[:, 2 * mix:3 * mix])
    v_ref[0] = v
    vb_ref[0] = v.astype(BF16)
    vt_ref[0] = v[:, 0:fox_w].T.astype(BF16)

    logit = _dot(h, wf_ref[...])[:, 0:nf] + bf_ref[...]
    lf_ref[0] = jnp.minimum(logit, 0.0) - _softplus_neg_abs(logit)

    logit_t = _dot_nt(wft_ref[...], h) + bft_ref[...]
    lf_t = jnp.minimum(logit_t, 0.0) - _softplus_neg_abs(logit_t)
    local = _dot_parts(_split3(lf_t), tri_ref[...])

    @pl.when(si == 0)
    def _():
        carry_ref[...] = jnp.zeros_like(carry_ref)

    cum = local + carry_ref[:, 0:1]
    cum_ref[0] = cum
    ts = cum.shape[1]
    carry_ref[...] = jnp.broadcast_to(cum[:, ts - 1:ts], carry_ref.shape)


def _inproj(x, sc, sh, w_qkv, wf, wft, bf, bft, *, mix, nf, hd):
    nb, s, d = x.shape
    mrows = sc.shape[1]
    ts = min(s, 512)
    assert s % ts == 0
    nfp = wft.shape[0]
    tri = _stack_rows((jnp.arange(ts)[:, None] <= jnp.arange(ts)[None, :]).astype(BF16), 3)
    col_head = jnp.arange(mix) // hd
    q_scale = (jnp.where(col_head < nf, LOG2E, 1.0) * float(hd) ** -0.5).astype(F32).reshape(1, mix)
    tok = lambda b, i: (b, i, 0)
    mod = lambda b, i: (b, 0, 0)
    const = lambda b, i: (0, 0)
    return pl.pallas_call(
        functools.partial(_inproj_kernel, mix=mix, nf=nf),
        out_shape=(jax.ShapeDtypeStruct((nb, s, mix), BF16),
                   jax.ShapeDtypeStruct((nb, s, mix), F32),
                   jax.ShapeDtypeStruct((nb, s, mix), F32),
                   jax.ShapeDtypeStruct((nb, s, mix), BF16),
                   jax.ShapeDtypeStruct((nb, s, mix), BF16),
                   jax.ShapeDtypeStruct((nb, s, nf), F32),
                   jax.ShapeDtypeStruct((nb, nfp, s), F32),
                   jax.ShapeDtypeStruct((nb, nf * hd, s), BF16),
                   jax.ShapeDtypeStruct((nb, nf * hd, s), BF16)),
        grid=(nb, s // ts),
        in_specs=[pl.BlockSpec((1, ts, d), tok),
                  pl.BlockSpec((1, mrows, d), mod),
                  pl.BlockSpec((1, mrows, d), mod),
                  pl.BlockSpec((d, 3 * mix), const),
                  pl.BlockSpec((d, V7X_LANES), const),
                  pl.BlockSpec((nfp, d), const),
                  pl.BlockSpec((1, nf), const),
                  pl.BlockSpec((nfp, 1), const),
                  pl.BlockSpec((3 * ts, ts), const),
                  pl.BlockSpec((1, mix), const)],
        out_specs=(pl.BlockSpec((1, ts, mix), tok),
                   pl.BlockSpec((1, ts, mix), tok),
                   pl.BlockSpec((1, ts, mix), tok),
                   pl.BlockSpec((1, ts, mix), tok),
                   pl.BlockSpec((1, ts, mix), tok),
                   pl.BlockSpec((1, ts, nf), tok),
                   pl.BlockSpec((1, nfp, ts), lambda b, i: (b, 0, i)),
                   pl.BlockSpec((1, nf * hd, ts), lambda b, i: (b, 0, i)),
                   pl.BlockSpec((1, nf * hd, ts), lambda b, i: (b, 0, i))),
        scratch_shapes=[pltpu.VMEM((nfp, V7X_LANES), F32)],
        compiler_params=pltpu.CompilerParams(
            dimension_semantics=("arbitrary", "arbitrary"),
            vmem_limit_bytes=V7X_VMEM_BUDGET),
        name="inproj",
    )(x, sc, sh, w_qkv, wf, wft, bf, bft, tri, q_scale)


def _pair_rmsnorm(outs, lane, hd, g):
    first = (lane // hd) == 0
    o = jnp.where(first, outs[0], outs[1])
    osq = o * o
    ms0 = jnp.sum(jnp.where(first, osq, 0.0), axis=-1, keepdims=True)
    ms1 = jnp.sum(jnp.where(first, 0.0, osq), axis=-1, keepdims=True)
    ms = jnp.where(first, ms0, ms1) * (1.0 / hd)
    return o * lax.rsqrt(ms + RMS_EPS) * g


def _fox_kernel(qt_ref, k_ref, vt_ref, cum_ref, g_ref, o_ref, ka_ref, va_ref, *, tq, hd, chunk):
    qi = pl.program_id(2)
    s_len = k_ref.shape[1]
    lane = lax.broadcasted_iota(jnp.int32, (1, V7X_LANES), 1)
    feat = lax.broadcasted_iota(jnp.int32, (V7X_LANES, 1), 0)

    @pl.when(qi == 0)
    def _():
        for c in range(s_len // chunk):
            rows = slice(c * chunk, (c + 1) * chunk)
            k = k_ref[0, rows, :].astype(F32)
            vt = vt_ref[0, :, rows].astype(F32)
            for h in range(2):
                spare = (1 - h) * hd
                ka = jnp.where((lane // hd) == h, k, 0.0)
                for i, part in enumerate(_split3(cum_ref[0, 0, rows, h:h + 1] * LOG2E)):
                    ka = jnp.where(lane == spare + i, part.astype(F32), ka)
                ka_ref[h, rows, :] = ka.astype(BF16)
                va_ref[h, :, rows] = jnp.where((feat // hd) == h, vt, 1.0).astype(BF16)

    featq = lax.broadcasted_iota(jnp.int32, (V7X_LANES, tq), 0)
    qtf = qt_ref[0].astype(F32)
    qa = []
    for h in range(2):
        spare = (1 - h) * hd
        fold = (featq >= spare) & (featq < spare + FOX_FOLD_PARTS)
        qa.append(jnp.where((featq // hd) == h, qtf, jnp.where(fold, -1.0, 0.0)).astype(BF16))
    krow = lax.broadcasted_iota(jnp.int32, (tq, tq), 0)
    qcol = lax.broadcasted_iota(jnp.int32, (tq, tq), 1)

    def step(kb, carry, masked):
        k0 = pl.multiple_of(kb * tq, tq)
        new = []
        for h in range(2):
            m, acc = carry[h]
            s = _dot(ka_ref[h, pl.ds(k0, tq), :], qa[h])
            if masked:
                s = jnp.where(krow <= qcol, s, -jnp.inf)
            m_new = jnp.maximum(m, jnp.max(s, axis=0, keepdims=True))
            alpha = jnp.exp2(m - m_new)
            p = jnp.exp2(s - m_new).astype(BF16)
            new.append((m_new, alpha * acc + _dot(va_ref[h, :, pl.ds(k0, tq)], p)))
        return tuple(new)

    init = tuple((jnp.full((1, tq), -jnp.inf, F32), jnp.zeros((V7X_LANES, tq), F32)) for _ in range(2))
    carry = lax.fori_loop(0, qi, functools.partial(step, masked=False), init)
    carry = step(qi, carry, True)
    first = (feat // hd) == 0
    den0 = carry[0][1][hd:hd + 1, :]
    den1 = carry[1][1][0:1, :]
    o = jnp.where(first, carry[0][1] / den0, carry[1][1] / den1)
    osq = o * o
    ms0 = jnp.sum(jnp.where(first, osq, 0.0), axis=0, keepdims=True)
    ms1 = jnp.sum(jnp.where(first, 0.0, osq), axis=0, keepdims=True)
    ms = jnp.where(first, ms0, ms1) * (1.0 / hd)
    o_ref[0] = (o * lax.rsqrt(ms + RMS_EPS) * g_ref[...]).T.astype(BF16)


def _sb_kernel(q_ref, k_ref, v_ref, tri_ref, g_ref, o_ref, *, tq, sub, hd):
    qi = pl.program_id(2)
    q = q_ref[0]
    lane = lax.broadcasted_iota(jnp.int32, (1, V7X_LANES), 1)
    row = lax.broadcasted_iota(jnp.int32, (tq, tq), 0)
    col = lax.broadcasted_iota(jnp.int32, (tq, tq), 1)
    qh = [jnp.where((lane // hd) == h, q, jnp.zeros_like(q)) for h in range(2)]

    def step(k0, nk, carry, qs, masked):
        n_sub = nk // sub
        k = k_ref[0, pl.ds(k0, nk), :]
        v = v_ref[0, pl.ds(k0, nk), :]
        tri = tri_ref[...]
        new = []
        for h in range(2):
            rsum, acc = carry[h]
            z = _dot_nt(qs[h], k)
            sp = jnp.maximum(z, 0.0) + _softplus_neg_abs(z)
            lsz = z - sp
            if masked:
                valid = col < row
                sp = jnp.where(valid, sp, 0.0)
            afters = []
            later = rsum
            for c in reversed(range(n_sub)):
                part = sp[:, c * sub:(c + 1) * sub]
                afters.append(_dot(part.astype(BF16), tri) + later)
                later = later + jnp.sum(part, axis=-1, keepdims=True)
            after = afters[0] if n_sub == 1 else jnp.concatenate(afters[::-1], axis=-1)
            a = jnp.exp(lsz - after)
            if masked:
                a = jnp.where(valid, a, 0.0)
            new.append((later, acc + _dot(a.astype(BF16), v)))
        return tuple(new)

    def least_mass(carry, first_row=0):
        return jnp.minimum(jnp.min(carry[0][0][first_row:]), jnp.min(carry[1][0][first_row:]))

    def walk_earlier(start, nk, carry, qs, first_row=0):
        n_units = qi * (tq // nk)

        def more_to_do(state):
            i, _, least = state
            return jnp.logical_and(i < n_units, least < SB_DEAD_MASS)

        def earlier_unit(state):
            i, c, _ = state
            c = step(pl.multiple_of(qi * tq - (i + 1) * nk, nk), nk, c, qs, False)
            return i + 1, c, least_mass(c, first_row)

        i, carry, _ = lax.while_loop(more_to_do, earlier_unit, (start, carry, least_mass(carry, first_row)))
        return i, carry

    init = tuple((jnp.zeros((tq, 1), F32), jnp.zeros((tq, V7X_LANES), F32)) for _ in range(2))
    carry = step(pl.multiple_of(qi * tq, tq), tq, init, qh, True)
    top = tq // 2
    i, carry = walk_earlier(jnp.int32(0), tq, carry, qh, first_row=top)
    _, upper = walk_earlier(i * (tq // sub), sub, tuple((r[:top], a[:top]) for (r, a) in carry),
                            [x[:top] for x in qh])
    carry = tuple((r, jnp.concatenate([ua, a[top:]], axis=0)) for (_, ua), (r, a) in zip(upper, carry))
    outs = [acc for (_, acc) in carry]
    o_ref[0] = _pair_rmsnorm(outs, lane, hd, g_ref[...]).astype(BF16)


def _prompt_attention(q, kb, vb, q_t, v_t, cum4, g_mix2, *, nf, hd):
    b, s, mix = q.shape
    nh = mix // hd
    fox_pairs = nf // 2
    sb_pairs = (nh - nf) // 2
    tq_fox = min(s, 1024)
    tq_sb = min(s, 512)
    sub = min(tq_sb, V7X_MXU_DIM)
    assert s % tq_fox == 0 and s % tq_sb == 0 and tq_sb % sub == 0
    tri = (jnp.arange(sub)[:, None] > jnp.arange(sub)[None, :]).astype(BF16)
    params = pltpu.CompilerParams(
        dimension_semantics=("arbitrary", "arbitrary", "arbitrary"),
        vmem_limit_bytes=V7X_VMEM_BUDGET)

    def specs(off, tq):
        return (pl.BlockSpec((1, tq, V7X_LANES), lambda bi, p, i: (bi, i, p + off)),
                pl.BlockSpec((1, s, V7X_LANES), lambda bi, p, i: (bi, 0, p + off)),
                pl.BlockSpec((1, s, V7X_LANES), lambda bi, p, i: (bi, 0, p + off)),
                pl.BlockSpec((1, V7X_LANES), lambda bi, p, i: (0, p + off)),
                pl.BlockSpec((1, tq, V7X_LANES), lambda bi, p, i: (bi, i, p)))

    fox_w = fox_pairs * V7X_LANES
    chunk = min(s, 512)
    assert s % chunk == 0
    cum_col = jnp.swapaxes(cum4, 2, 3)
    o_fox = pl.pallas_call(
        functools.partial(_fox_kernel, tq=tq_fox, hd=hd, chunk=chunk),
        out_shape=jax.ShapeDtypeStruct((b, s, fox_w), BF16),
        grid=(b, fox_pairs, s // tq_fox),
        in_specs=[pl.BlockSpec((1, V7X_LANES, tq_fox), lambda bi, p, i: (bi, p, i)),
                  pl.BlockSpec((1, s, V7X_LANES), lambda bi, p, i: (bi, 0, p)),
                  pl.BlockSpec((1, V7X_LANES, s), lambda bi, p, i: (bi, p, 0)),
                  pl.BlockSpec((1, 1, s, 2), lambda bi, p, i: (bi, p, 0, 0)),
                  pl.BlockSpec((V7X_LANES, 1), lambda bi, p, i: (p, 0))],
        out_specs=pl.BlockSpec((1, tq_fox, V7X_LANES), lambda bi, p, i: (bi, i, p)),
        scratch_shapes=[pltpu.VMEM((2, s, V7X_LANES), BF16), pltpu.VMEM((2, V7X_LANES, s), BF16)],
        compiler_params=params,
        name="fox_prompt",
    )(q_t, kb, v_t, cum_col, g_mix2.reshape(mix, 1))
    qs, ks, vs, gs, os_ = specs(fox_pairs, tq_sb)
    o_sb = pl.pallas_call(
        functools.partial(_sb_kernel, tq=tq_sb, sub=sub, hd=hd),
        out_shape=jax.ShapeDtypeStruct((b, s, sb_pairs * V7X_LANES), BF16),
        grid=(b, sb_pairs, s // tq_sb),
        in_specs=[qs, ks, vs, pl.BlockSpec((sub, sub), lambda bi, p, i: (0, 0)), gs],
        out_specs=os_,
        compiler_params=params,
        name="sb_prompt",
    )(q, kb, vb, tri, g_mix2)
    return o_fox, o_sb


def _decode_kernel(pt_ref, q_ref, knt_ref, vnt_ref, lfn_ref, tri_ref, g_ref, *rest,
                   pps, n_tok, nh, nf, hd):
    del pt_ref
    k_refs = rest[0:pps]
    v_refs = rest[pps:2 * pps]
    lf_refs = rest[2 * pps:3 * pps]
    o_ref = rest[3 * pps]
    qbd_ref, m_ref, l_ref, r_ref, gc_ref, acc_ref = rest[3 * pps + 1:]
    j = pl.program_id(1)
    rows = n_tok * nh
    ps = tri_ref.shape[1]
    mix = nh * hd
    rowi = lax.broadcasted_iota(jnp.int32, (rows, 1), 0)
    row_h = rowi % nh
    row_t = rowi // nh
    is_fox = row_h < nf
    key = lax.broadcasted_iota(jnp.int32, (1, ps), 1)
    lane = lax.broadcasted_iota(jnp.int32, (1, mix), 1)

    def process(blocks, new):
        tri = tri_ref[...]
        qbd = qbd_ref[...]
        g_off = gc_ref[0:nf, 0:1]
        r_off = r_ref[:, 0:1]
        n_blk = len(blocks)
        ss = [_dot(qbd, k_t) for k_t, _, _ in blocks]
        lfs = [lf_t[0:nf] for _, _, lf_t in blocks]
        sps = [jnp.maximum(s, 0.0) + _softplus_neg_abs(s) for s in ss]
        lszs = [s - sp for s, sp in zip(ss, sps)]
        if new:
            valid = key < row_t
            sps = [jnp.where(valid, sp, 0.0) for sp in sps]
        lf_sfx = _dot_parts(_split3(lfs[0] if n_blk == 1 else jnp.concatenate(lfs, axis=0)), tri)
        sp_all = sps[0] if n_blk == 1 else jnp.concatenate(sps, axis=0)
        sp_sfx = _dot(sp_all.astype(BF16), tri[0:ps])
        xs, a_sbs = [], []
        for i in range(n_blk):
            g_suffix = (lf_sfx[i * nf:(i + 1) * nf] + g_off) * LOG2E
            g_off = g_off + jnp.sum(lfs[i], axis=-1, keepdims=True)
            if nh > nf:
                g_suffix = jnp.concatenate([g_suffix, jnp.zeros((nh - nf, ps), F32)], axis=0)
            x = ss[i] + jnp.concatenate([g_suffix] * n_tok, axis=0)
            if new:
                x = jnp.where((key < n_tok) & (key <= row_t), x, -jnp.inf)
            xs.append(x)
            a_sb = jnp.exp(lszs[i] - (sp_sfx[i * rows:(i + 1) * rows] + r_off))
            if new:
                a_sb = jnp.where(valid, a_sb, 0.0)
            a_sbs.append(a_sb)
            r_off = r_off + jnp.sum(sps[i], axis=-1, keepdims=True)
        gc_ref[0:nf, :] = jnp.broadcast_to(g_off, (nf, gc_ref.shape[1]))
        r_ref[...] = jnp.broadcast_to(r_off, r_ref.shape)
        m_prev = m_ref[:, 0:1]
        m_new = m_prev
        for x in xs:
            m_new = jnp.maximum(m_new, jnp.max(x, axis=-1, keepdims=True))
        alpha = jnp.exp2(m_prev - m_new)
        l_new = alpha * l_ref[:, 0:1]
        pv = None
        for x, a_sb, (_, v_t, _) in zip(xs, a_sbs, blocks):
            p_fox = jnp.exp2(x - m_new)
            l_new = l_new + jnp.sum(p_fox, axis=-1, keepdims=True)
            d = _dot_nt(jnp.where(is_fox, p_fox, a_sb).astype(BF16), v_t)
            pv = d if pv is None else pv + d
        m_ref[...] = jnp.broadcast_to(m_new, m_ref.shape)
        l_ref[...] = jnp.broadcast_to(l_new, l_ref.shape)
        acc_ref[...] = acc_ref[...] * jnp.where(is_fox, alpha, 1.0) + pv

    @pl.when(j == 0)
    def _():
        q = q_ref[0]
        head_rows = lax.broadcasted_iota(jnp.int32, (nh, 1), 0)
        own = (lane // hd) == head_rows
        qbd_ref[...] = jnp.concatenate(
            [jnp.where(own, jnp.broadcast_to(q[t:t + 1], (nh, mix)), 0.0) for t in range(n_tok)],
            axis=0).astype(BF16)
        m_ref[...] = jnp.full(m_ref.shape, -jnp.inf, F32)
        l_ref[...] = jnp.zeros_like(l_ref)
        r_ref[...] = jnp.zeros_like(r_ref)
        gc_ref[...] = jnp.zeros_like(gc_ref)
        acc_ref[...] = jnp.zeros_like(acc_ref)
        process([(knt_ref[0], vnt_ref[0], lfn_ref[0])], True)

    process([(k_refs[i][0, 0].astype(BF16), v_refs[i][0, 0].astype(BF16), lf_refs[i][0, 0])
             for i in range(pps)], False)

    @pl.when(j == pl.num_programs(1) - 1)
    def _():
        o = acc_ref[...] * jnp.where(is_fox, 1.0 / l_ref[:, 0:1], 1.0)
        own = (lane // hd) == row_h
        ms = jnp.sum(jnp.where(own, o * o, 0.0), axis=-1, keepdims=True) * (1.0 / hd)
        on = jnp.where(own, o * lax.rsqrt(ms + RMS_EPS), 0.0)
        outs = [jnp.sum(on[t * nh:(t + 1) * nh], axis=0, keepdims=True) for t in range(n_tok)]
        o_ref[0] = (jnp.concatenate(outs, axis=0) * g_ref[...]).astype(BF16)


def _decode_attention(q, k_new_t, v_new_t, lf_new_t, cache_kt, cache_vt, cache_lf_t, page_table,
                      g_mix2, *, layer, nh, nf, hd):
    db, n_tok, mix = q.shape
    n_pages = page_table.shape[1]
    nfp, ps = cache_lf_t.shape[2:]
    pps = max(p for p in (16, 8, 4, 2, 1) if n_pages % p == 0)
    rows = n_tok * nh
    tri = _stack_rows((jnp.arange(ps)[:, None] > jnp.arange(ps)[None, :]).astype(BF16), 3)
    per_b = lambda b, j, pt: (b, 0, 0)
    const = lambda b, j, pt: (0, 0)

    def page_map(i):
        return lambda b, j, pt: (pt[b, n_pages - 1 - (j * pps + i)], layer, 0, 0)

    in_specs = [pl.BlockSpec((1, n_tok, mix), per_b),
                pl.BlockSpec((1, mix, ps), per_b),
                pl.BlockSpec((1, mix, ps), per_b),
                pl.BlockSpec((1, nfp, ps), per_b),
                pl.BlockSpec((3 * ps, ps), const),
                pl.BlockSpec((1, mix), const)]
    in_specs += [pl.BlockSpec((1, 1, mix, ps), page_map(i)) for i in range(pps)]
    in_specs += [pl.BlockSpec((1, 1, mix, ps), page_map(i)) for i in range(pps)]
    in_specs += [pl.BlockSpec((1, 1, nfp, ps), page_map(i)) for i in range(pps)]
    grid_spec = pltpu.PrefetchScalarGridSpec(
        num_scalar_prefetch=1,
        grid=(db, n_pages // pps),
        in_specs=in_specs,
        out_specs=pl.BlockSpec((1, n_tok, mix), per_b),
        scratch_shapes=[pltpu.VMEM((rows, mix), BF16),
                        pltpu.VMEM((rows, V7X_LANES), F32),
                        pltpu.VMEM((rows, V7X_LANES), F32),
                        pltpu.VMEM((rows, V7X_LANES), F32),
                        pltpu.VMEM((nfp, V7X_LANES), F32),
                        pltpu.VMEM((rows, mix), F32)])
    return pl.pallas_call(
        functools.partial(_decode_kernel, pps=pps, n_tok=n_tok, nh=nh, nf=nf, hd=hd),
        out_shape=jax.ShapeDtypeStruct((db, n_tok, mix), BF16),
        grid_spec=grid_spec,
        compiler_params=pltpu.CompilerParams(
            dimension_semantics=("arbitrary", "arbitrary"),
            vmem_limit_bytes=V7X_VMEM_BUDGET),
        name="decode_attn",
    )(page_table, q, k_new_t, v_new_t, lf_new_t, tri, g_mix2,
      *([cache_kt] * pps), *([cache_vt] * pps), *([cache_lf_t] * pps))


def _outproj_kernel(*refs, n_parts, alpha, ng, epg):
    mix_refs = refs[:n_parts]
    (x_ref, g1_ref, sc_ref, sh_ref, w_ref, lg_ref, lb_ref, wr_ref, br_ref,
     x1_ref, h2_ref, gate_ref) = refs[n_parts:]
    mo, off = None, 0
    for part in mix_refs:
        width = part.shape[2]
        term = _dot(part[0], w_ref[off:off + width, :])
        mo = term if mo is None else mo + term
        off += width
    x1 = _layer_norm(alpha * x_ref[0] + g1_ref[0] * mo, lg_ref[...], lb_ref[...])
    x1_ref[0] = x1
    h2 = x1 * (1.0 + sc_ref[0]) + sh_ref[0]
    h2_ref[0] = h2.astype(BF16)
    h_hi, h_lo = _split2(h2)
    logits = _dot(jnp.concatenate([h_hi, h_lo, h_hi], axis=-1), wr_ref[...]) + br_ref[...]
    lane = lax.broadcasted_iota(jnp.int32, logits.shape, 1)
    big = jnp.int32(2 ** 30)
    neg = -jnp.inf
    gl = jnp.where(lane < ng, logits, neg)
    gmax = jnp.max(gl, axis=-1, keepdims=True)
    gidx = jnp.min(jnp.where(gl == gmax, lane, big), axis=-1, keepdims=True)
    g_w = 1.0 / jnp.sum(jnp.exp(gl - gmax), axis=-1, keepdims=True)
    in_group = (lane >= ng + gidx * epg) & (lane < ng + (gidx + 1) * epg)
    el = jnp.where(in_group, logits, neg)
    m1 = jnp.max(el, axis=-1, keepdims=True)
    i1 = jnp.min(jnp.where(el == m1, lane, big), axis=-1, keepdims=True)
    el2 = jnp.where(lane == i1, neg, el)
    m2 = jnp.max(el2, axis=-1, keepdims=True)
    i2 = jnp.min(jnp.where(el2 == m2, lane, big), axis=-1, keepdims=True)
    den = jnp.sum(jnp.exp(el - m1), axis=-1, keepdims=True)
    p1 = 1.0 / den
    p2 = jnp.exp(m2 - m1) / den
    w1 = p1 / (p1 + p2)
    w2 = p2 / (p1 + p2)
    gate_ref[0] = jnp.where(lane == i1, g_w * w1, jnp.where(lane == i2, g_w * w2, 0.0))


def _outproj(mix_parts, x, g1, sc2, sh2, w_out, ln_g, ln_b, w_r, b_r, *, alpha, ng, epg):
    nb, s, d = x.shape
    mix = sum(part.shape[2] for part in mix_parts)
    mrows = g1.shape[1]
    ts = min(s, 512)
    assert s % ts == 0
    tok = lambda b, i: (b, i, 0)
    mod = lambda b, i: (b, 0, 0)
    const = lambda b, i: (0, 0)
    return pl.pallas_call(
        functools.partial(_outproj_kernel, n_parts=len(mix_parts), alpha=alpha, ng=ng, epg=epg),
        out_shape=(jax.ShapeDtypeStruct((nb, s, d), F32),
                   jax.ShapeDtypeStruct((nb, s, d), BF16),
                   jax.ShapeDtypeStruct((nb, s, V7X_LANES), F32)),
        grid=(nb, s // ts),
        in_specs=[pl.BlockSpec((1, ts, part.shape[2]), tok) for part in mix_parts] + [
                  pl.BlockSpec((1, ts, d), tok),
                  pl.BlockSpec((1, mrows, d), mod),
                  pl.BlockSpec((1, mrows, d), mod),
                  pl.BlockSpec((1, mrows, d), mod),
                  pl.BlockSpec((mix, d), const),
                  pl.BlockSpec((1, d), const),
                  pl.BlockSpec((1, d), const),
                  pl.BlockSpec((3 * d, V7X_LANES), const),
                  pl.BlockSpec((1, V7X_LANES), const)],
        out_specs=(pl.BlockSpec((1, ts, d), tok),
                   pl.BlockSpec((1, ts, d), tok),
                   pl.BlockSpec((1, ts, V7X_LANES), tok)),
        compiler_params=pltpu.CompilerParams(
            dimension_semantics=("arbitrary", "arbitrary"),
            vmem_limit_bytes=V7X_VMEM_BUDGET),
        name="outproj_router",
    )(*mix_parts, x, g1, sc2, sh2, w_out, ln_g, ln_b, w_r, b_r)


def _moe_kernel(h_ref, gate_ref, x1_ref, g2_ref, wg_ref, wu_ref, wd_ref, lg_ref, lb_ref,
                y_ref, acc_ref, *, alpha, ng, ep):
    e = pl.program_id(2)

    @pl.when(e == 0)
    def _():
        acc_ref[...] = jnp.zeros_like(acc_ref)

    h = h_ref[0]
    gates = gate_ref[0]
    lane = lax.broadcasted_iota(jnp.int32, gates.shape, 1)
    zs = []
    for i in range(ep):
        a = _dot(h, wg_ref[i])
        u = _dot(h, wu_ref[i])
        gcol = jnp.sum(jnp.where(lane == ng + e * ep + i, gates, 0.0), axis=-1, keepdims=True)
        zs.append((a * (1.0 / (1.0 + jnp.exp(-a))) * u * gcol).astype(BF16))
    zz = zs[0] if ep == 1 else jnp.concatenate(zs, axis=-1)
    acc_ref[...] += _dot(zz, wd_ref[...])

    @pl.when(e == pl.num_programs(2) - 1)
    def _():
        y_ref[0] = _layer_norm(alpha * x1_ref[0] + g2_ref[0] * acc_ref[...],
                               lg_ref[...], lb_ref[...])


def _moe(h2, gates, x1, g2, w_gate, w_up, w_down2, ln_g, ln_b, *, alpha, ng):
    nb, s, d = x1.shape
    ne, _, f = w_gate.shape
    mrows = g2.shape[1]
    tm = min(s, 1024)
    assert s % tm == 0
    ep = 4 if ne % 4 == 0 else 1
    tok = lambda b, i, e: (b, i, 0)
    mod = lambda b, i, e: (b, 0, 0)
    const = lambda b, i, e: (0, 0)
    return pl.pallas_call(
        functools.partial(_moe_kernel, alpha=alpha, ng=ng, ep=ep),
        out_shape=jax.ShapeDtypeStruct((nb, s, d), F32),
        grid=(nb, s // tm, ne // ep),
        in_specs=[pl.BlockSpec((1, tm, d), tok),
                  pl.BlockSpec((1, tm, V7X_LANES), tok),
                  pl.BlockSpec((1, tm, d), tok),
                  pl.BlockSpec((1, mrows, d), mod),
                  pl.BlockSpec((ep, d, f), lambda b, i, e: (e, 0, 0)),
                  pl.BlockSpec((ep, d, f), lambda b, i, e: (e, 0, 0)),
                  pl.BlockSpec((ep * f, d), lambda b, i, e: (e, 0)),
                  pl.BlockSpec((1, d), const),
                  pl.BlockSpec((1, d), const)],
        out_specs=pl.BlockSpec((1, tm, d), tok),
        scratch_shapes=[pltpu.VMEM((tm, d), F32)],
        compiler_params=pltpu.CompilerParams(
            dimension_semantics=("arbitrary", "arbitrary", "arbitrary"),
            vmem_limit_bytes=V7X_VMEM_BUDGET),
        name="moe_experts",
    )(h2, gates, x1, g2, w_gate, w_up, w_down2, ln_g, ln_b)


def _pad_to(a, axis, size):
    if a.shape[axis] == size:
        return a
    pad = [(0, 0)] * a.ndim
    pad[axis] = (0, size - a.shape[axis])
    return jnp.pad(a, pad)


def kernel(x_prompt, x_sample, c_prompt, c_sample, cache_k, cache_v, cache_logf, page_table, w_in, b_forget, g_mix, w_out, w_ada, b_ada, ln1_g, ln1_b, ln2_g, ln2_b, w_router_group, b_router_group, w_router_expert, b_router_expert, w_gate_e, w_up_e, w_down_e):
    b, s, d = x_prompt.shape
    db, n_tok, _ = x_sample.shape
    depth = w_in.shape[0]
    n_pool, _, ps, nh, hd = cache_k.shape
    nf = b_forget.shape[1]
    mix = nh * hd
    ng = w_router_group.shape[2]
    ne = w_router_expert.shape[2]
    epg = ne // ng
    f = w_gate_e.shape[3]
    nfp = _round_up(nf, V7X_SUBLANES)
    alpha = float((2 * depth) ** 0.25)
    assert nf % 2 == 0 and (nh - nf) % 2 == 0 and 2 * hd == V7X_LANES
    assert ng + ne <= V7X_LANES and n_tok <= ps

    cache_kt = jnp.transpose(cache_k, (0, 1, 3, 4, 2)).reshape(n_pool, depth, mix, ps)
    cache_vt = jnp.transpose(cache_v, (0, 1, 3, 4, 2)).reshape(n_pool, depth, mix, ps)
    cache_lf_t = _pad_to(jnp.swapaxes(cache_logf, 2, 3), 2, nfp)

    n_c = b + db
    c_all = _pad_to(jnp.concatenate([c_prompt, c_sample], axis=0), 0, _round_up(n_c, V7X_SUBLANES))

    yp, ys = x_prompt, x_sample.reshape(1, db * n_tok, d)
    outs = {name: [] for name in ("kp", "vp", "lp", "ks", "vs", "ls")}
    for l in range(depth):
        mods = _adaln(c_all, w_ada[l], b_ada[l])
        mods_p = [m.reshape(b, 1, d) for m in jnp.split(mods[:b], 6, axis=-1)]
        mods_s = [jnp.repeat(m, n_tok, axis=0).reshape(1, db * n_tok, d)
                  for m in jnp.split(mods[b:n_c], 6, axis=-1)]

        w_qkv = w_in[l][:, :3 * mix].astype(BF16)
        w_f = w_in[l][:, 3 * mix:]
        wf = _pad_to(w_f, 1, V7X_LANES).astype(BF16)
        wft = _pad_to(w_f.T, 0, nfp).astype(BF16)
        bf = b_forget[l].reshape(1, nf)
        bft = _pad_to(b_forget[l].reshape(nf, 1), 0, nfp)
        g_mix2 = g_mix[l].reshape(1, mix)
        w_out_b = w_out[l].astype(BF16)
        w_r = _pad_to(jnp.concatenate([w_router_group[l], w_router_expert[l]], axis=1), 1, V7X_LANES)
        w_r_hi, w_r_lo = _split2(w_r)
        w_r = jnp.concatenate([w_r_hi, w_r_hi, w_r_lo], axis=0)
        b_r = _pad_to(jnp.concatenate([b_router_group[l], b_router_expert[l]]).reshape(1, ng + ne),
                      1, V7X_LANES)
        wg_b = w_gate_e[l].astype(BF16)
        wu_b = w_up_e[l].astype(BF16)
        wd_b = w_down_e[l].astype(BF16).reshape(ne * f, d)
        lg1, lb1 = ln1_g[l].reshape(1, d), ln1_b[l].reshape(1, d)
        lg2, lb2 = ln2_g[l].reshape(1, d), ln2_b[l].reshape(1, d)
        proj = functools.partial(_inproj, w_qkv=w_qkv, wf=wf, wft=wft, bf=bf, bft=bft,
                                 mix=mix, nf=nf, hd=hd)
        ffn = functools.partial(_moe, w_gate=wg_b, w_up=wu_b, w_down2=wd_b, ln_g=lg2, ln_b=lb2,
                                alpha=alpha, ng=ng)
        mixer_out = functools.partial(_outproj, w_out=w_out_b, ln_g=lg1, ln_b=lb1, w_r=w_r, b_r=b_r,
                                      alpha=alpha, ng=ng, epg=epg)

        sh1, sc1, g1, sh2, sc2, g2 = mods_p
        q, k, v, kb, vb, lf, cum, q_t, v_t = proj(yp, sc1, sh1)
        cum4 = cum[:, :nf].reshape(b, nf // 2, 2, s)
        mixn = _prompt_attention(q, kb, vb, q_t, v_t, cum4, g_mix2, nf=nf, hd=hd)
        x1, h2, gates = mixer_out(mixn, yp, g1, sc2, sh2)
        yp = ffn(h2, gates, x1, g2)
        outs["kp"].append(k.reshape(b, s, nh, hd))
        outs["vp"].append(v.reshape(b, s, nh, hd))
        outs["lp"].append(lf)

        sh1, sc1, g1, sh2, sc2, g2 = mods_s
        q, k, v, kb, vb, lf, _, _, _ = proj(ys, sc1, sh1)
        q_s = q.reshape(db, n_tok, mix).astype(F32)
        k_new_t = _pad_to(jnp.swapaxes(kb.reshape(db, n_tok, mix), 1, 2), 2, ps)
        v_new_t = _pad_to(jnp.swapaxes(vb.reshape(db, n_tok, mix), 1, 2), 2, ps)
        lf_new_t = _pad_to(_pad_to(jnp.swapaxes(lf.reshape(db, n_tok, nf), 1, 2), 1, nfp), 2, ps)
        mixn = _decode_attention(q_s, k_new_t, v_new_t, lf_new_t, cache_kt, cache_vt, cache_lf_t,
                                 page_table, g_mix2, layer=l, nh=nh, nf=nf, hd=hd)
        x1, h2, gates = mixer_out((mixn.reshape(1, db * n_tok, mix),), ys, g1, sc2, sh2)
        ys = ffn(h2, gates, x1, g2)
        outs["ks"].append(k.reshape(db, n_tok, nh, hd))
        outs["vs"].append(v.reshape(db, n_tok, nh, hd))
        outs["ls"].append(lf.reshape(db, n_tok, nf))

    stack = lambda name: jnp.stack(outs[name], axis=1)
    return (yp, ys.reshape(db, n_tok, d), stack("kp"), stack("vp"), stack("lp"),
            stack("ks"), stack("vs"), stack("ls"))
```
